```python
import jax, jax.numpy as jnp
from jax import lax
import numpy as np

D_MODEL = 1024
BATCH = 32
SEQ = 2048
DEPTH = 1
DEC_BATCH = 32
DEC_SEQ = 32
PAST_LEN = 4096

CHUNK = 64
HEAD_DIM = 64
N_HEADS = D_MODEL // HEAD_DIM
N_HEADS_A = N_HEADS // 2
N_HEADS_B = N_HEADS - N_HEADS_A
D_A = N_HEADS_A * HEAD_DIM
D_B = N_HEADS_B * HEAD_DIM
D_MIX = D_A + D_B
D_IN = 3 * D_A + 3 * D_B + N_HEADS_B
IN_SPLITS = (D_A, 2 * D_A, 3 * D_A, 3 * D_A + D_B, 3 * D_A + 2 * D_B, 3 * D_A + 3 * D_B)
A_LEFT_CHUNKS = 8
A_WINDOW = A_LEFT_CHUNKS * CHUNK
A_BAND = A_WINDOW + CHUNK
REL_CLIP = 128
Q_BLOCK = 128
D_FF = ((8 * D_MODEL + 3 * 256 - 1) // (3 * 256)) * 256
PLE_DIM = 256
FORGET_BIAS_INIT = 3.0
ATTN_SCALE = HEAD_DIM ** -0.5
NEG_INF = -1e30
EPS = 1e-6

kernel_name = "hybrid_chunk_band_fox_streaming_step"


def rms_norm(x, g):
    xf = x.astype(jnp.float32)
    y = xf * lax.rsqrt(jnp.mean(xf * xf, axis=-1, keepdims=True) + EPS)
    return (y * g.astype(jnp.float32)).astype(x.dtype)


def _split_heads(x, n_heads):
    return x.reshape(*x.shape[:-1], n_heads, HEAD_DIM)


def _rel_bias(table, rel):
    return table[:, jnp.clip(rel, -REL_CLIP, REL_CLIP) + REL_CLIP].astype(jnp.float32)


def _mix_projections(h, w_in, b_f, qn_a, kn_a, qn_b, kn_b):
    proj = h @ w_in
    q_a, k_a, v_a, q_b, k_b, v_b, g_f = jnp.split(proj, IN_SPLITS, axis=-1)
    q_a = rms_norm(_split_heads(q_a, N_HEADS_A), qn_a)
    k_a = rms_norm(_split_heads(k_a, N_HEADS_A), kn_a)
    v_a = _split_heads(v_a, N_HEADS_A)
    q_b = rms_norm(_split_heads(q_b, N_HEADS_B), qn_b)
    k_b = rms_norm(_split_heads(k_b, N_HEADS_B), kn_b)
    v_b = _split_heads(v_b, N_HEADS_B)
    logf = jax.nn.log_sigmoid(g_f.astype(jnp.float32) + b_f.astype(jnp.float32))
    return q_a, k_a, v_a, q_b, k_b, v_b, logf


def chunk_band_attention_prompt(q, k, v, rel_table):
    b, t, h, d = q.shape
    n_c = t // CHUNK
    pad = ((0, 0), (A_WINDOW, 0), (0, 0), (0, 0))
    k_pad = jnp.pad(k, pad).reshape(b, n_c + A_LEFT_CHUNKS, CHUNK, h, d)
    v_pad = jnp.pad(v, pad).reshape(b, n_c + A_LEFT_CHUNKS, CHUNK, h, d)
    k_band = jnp.concatenate([k_pad[:, j:j + n_c] for j in range(A_LEFT_CHUNKS + 1)], axis=2)
    v_band = jnp.concatenate([v_pad[:, j:j + n_c] for j in range(A_LEFT_CHUNKS + 1)], axis=2)
    q_c = q.reshape(b, n_c, CHUNK, h, d)
    s = jnp.einsum("bcqhd,bckhd->bchqk", q_c, k_band).astype(jnp.float32) * ATTN_SCALE
    rel = (A_WINDOW + jnp.arange(CHUNK))[:, None] - jnp.arange(A_BAND)[None, :]
    s = s + _rel_bias(rel_table, rel)[None, None]
    valid = (jnp.arange(n_c)[:, None] + jnp.arange(A_BAND)[None, :] // CHUNK) >= A_LEFT_CHUNKS
    s = jnp.where(valid[None, :, None, None, :], s, NEG_INF)
    p = jax.nn.softmax(s, axis=-1).astype(v.dtype)
    o = jnp.einsum("bchqk,bckhd->bcqhd", p, v_band)
    return o.reshape(b, t, h, d)


def chunk_band_attention_sample(q, k, v, cache_k, cache_v, rel_table):
    w = cache_k.shape[1]
    t = q.shape[1]
    k_all = jnp.concatenate([cache_k.astype(k.dtype), k], axis=1)
    v_all = jnp.concatenate([cache_v.astype(v.dtype), v], axis=1)
    s = jnp.einsum("bqhd,bkhd->bhqk", q, k_all).astype(jnp.float32) * ATTN_SCALE
    rel = (w + jnp.arange(t))[:, None] - jnp.arange(w + t)[None, :]
    s = s + _rel_bias(rel_table, rel)[None]
    p = jax.nn.softmax(s, axis=-1).astype(v.dtype)
    return jnp.einsum("bhqk,bkhd->bqhd", p, v_all)


def forgetting_attention_prompt(q, k, v, logf):
    b, t, h, d = q.shape
    c_t = jnp.cumsum(logf, axis=1).transpose(0, 2, 1)
    k_pos = jnp.arange(t)

    def block(i):
        start = i * Q_BLOCK
        q_blk = lax.dynamic_slice_in_dim(q, start, Q_BLOCK, axis=1)
        c_q = lax.dynamic_slice_in_dim(c_t, start, Q_BLOCK, axis=2)
        q_pos = start + jnp.arange(Q_BLOCK)
        s = jnp.einsum("bqhd,bkhd->bhqk", q_blk, k).astype(jnp.float32) * ATTN_SCALE
        s = s + (c_q[:, :, :, None] - c_t[:, :, None, :])
        s = jnp.where((k_pos[None, :] <= q_pos[:, None])[None, None], s, NEG_INF)
        p = jax.nn.softmax(s, axis=-1).astype(v.dtype)
        return jnp.einsum("bhqk,bkhd->bqhd", p, v)

    o = lax.map(block, jnp.arange(t // Q_BLOCK))
    return o.transpose(1, 0, 2, 3, 4).reshape(b, t, h, d)


def forgetting_attention_sample(q, k, v, logf, cache_k, cache_v, cache_logf):
    past = cache_k.shape[1]
    t = q.shape[1]
    k_all = jnp.concatenate([cache_k.astype(k.dtype), k], axis=1)
    v_all = jnp.concatenate([cache_v.astype(v.dtype), v], axis=1)
    lf_all = jnp.concatenate([cache_logf.astype(jnp.float32), logf], axis=1)
    c_t = jnp.cumsum(lf_all, axis=1).transpose(0, 2, 1)
    c_q = c_t[:, :, past:]
    s = jnp.einsum("bqhd,bkhd->bhqk", q, k_all).astype(jnp.float32) * ATTN_SCALE
    s = s + (c_q[:, :, :, None] - c_t[:, :, None, :])
    causal = jnp.arange(past + t)[None, :] <= (past + jnp.arange(t))[:, None]
    s = jnp.where(causal[None, None], s, NEG_INF)
    p = jax.nn.softmax(s, axis=-1).astype(v.dtype)
    return jnp.einsum("bhqk,bkhd->bqhd", p, v_all)


def _layer_tail(x, o_a, o_b, p, w_out, norm_ffn, w_gate, w_up, w_down, norm_ple, w_ple_gate, w_ple_proj):
    b, t = x.shape[:2]
    o = jnp.concatenate([o_a.reshape(b, t, D_A), o_b.reshape(b, t, D_B)], axis=-1) @ w_out
    x = x + o
    h = rms_norm(x, norm_ffn)
    x = x + (jax.nn.silu(h @ w_gate) * (h @ w_up)) @ w_down
    gate = jax.nn.sigmoid(rms_norm(x, norm_ple) @ w_ple_gate)
    return x + (p @ w_ple_proj) * gate


def setup_inputs(seed: int = 0) -> dict:
    key = jax.random.key(seed)
    ks = jax.random.split(key, 32)
    f32 = jnp.float32

    def nrm(k, shape, scale):
        return scale * jax.random.normal(k, shape, f32)

    w_a = min(A_WINDOW, PAST_LEN)
    return {
        "x_prompt": nrm(ks[0], (BATCH, SEQ, D_MODEL), 1.0),
        "x_sample": nrm(ks[1], (DEC_BATCH, DEC_SEQ, D_MODEL), 1.0),
        "cache_k_a": nrm(ks[2], (DEPTH, DEC_BATCH, w_a, N_HEADS_A, HEAD_DIM), 1.0),
        "cache_v_a": nrm(ks[3], (DEPTH, DEC_BATCH, w_a, N_HEADS_A, HEAD_DIM), 1.0),
        "cache_k_b": nrm(ks[4], (DEPTH, DEC_BATCH, PAST_LEN, N_HEADS_B, HEAD_DIM), 1.0),
        "cache_v_b": nrm(ks[5], (DEPTH, DEC_BATCH, PAST_LEN, N_HEADS_B, HEAD_DIM), 1.0),
        "cache_logf_b": jax.nn.log_sigmoid(FORGET_BIAS_INIT + nrm(ks[6], (DEPTH, DEC_BATCH, PAST_LEN, N_HEADS_B), 1.0)),
        "p_prompt": nrm(ks[7], (DEPTH, BATCH, SEQ, PLE_DIM), 1.0),
        "p_sample": nrm(ks[8], (DEPTH, DEC_BATCH, DEC_SEQ, PLE_DIM), 1.0),
        "norm_mix": 1.0 + nrm(ks[9], (DEPTH, D_MODEL), 0.05),
        "w_in": nrm(ks[10], (DEPTH, D_MODEL, D_IN), D_MODEL ** -0.5),
        "b_f": FORGET_BIAS_INIT + nrm(ks[11], (DEPTH, N_HEADS_B), 0.5),
        "q_norm_a": 1.0 + nrm(ks[12], (DEPTH, HEAD_DIM), 0.05),
        "k_norm_a": 1.0 + nrm(ks[13], (DEPTH, HEAD_DIM), 0.05),
        "q_norm_b": 1.0 + nrm(ks[14], (DEPTH, HEAD_DIM), 0.05),
        "k_norm_b": 1.0 + nrm(ks[15], (DEPTH, HEAD_DIM), 0.05),
        "rel_bias_a": nrm(ks[16], (DEPTH, N_HEADS_A, 2 * REL_CLIP + 1), 0.3),
        "w_out": nrm(ks[17], (DEPTH, D_MIX, D_MODEL), D_MIX ** -0.5),
        "norm_ffn": 1.0 + nrm(ks[18], (DEPTH, D_MODEL), 0.05),
        "w_gate": nrm(ks[19], (DEPTH, D_MODEL, D_FF), D_MODEL ** -0.5),
        "w_up": nrm(ks[20], (DEPTH, D_MODEL, D_FF), D_MODEL ** -0.5),
        "w_down": nrm(ks[21], (DEPTH, D_FF, D_MODEL), D_FF ** -0.5),
        "norm_ple": 1.0 + nrm(ks[22], (DEPTH, D_MODEL), 0.05),
        "w_ple_gate": nrm(ks[23], (DEPTH, D_MODEL, D_MODEL), D_MODEL ** -0.5),
        "w_ple_proj": nrm(ks[24], (DEPTH, PLE_DIM, D_MODEL), PLE_DIM ** -0.5),
    }


def reference(x_prompt, x_sample, cache_k_a, cache_v_a, cache_k_b, cache_v_b, cache_logf_b,
              p_prompt, p_sample, norm_mix, w_in, b_f, q_norm_a, k_norm_a, q_norm_b, k_norm_b,
              rel_bias_a, w_out, norm_ffn, w_gate, w_up, w_down, norm_ple, w_ple_gate, w_ple_proj):
    xp = x_prompt
    xs = x_sample
    ka_p, va_p, kb_p, vb_p, lf_p = [], [], [], [], []
    ka_s, va_s, kb_s, vb_s, lf_s = [], [], [], [], []
    for i in range(DEPTH):
        tail_w = (w_out[i], norm_ffn[i], w_gate[i], w_up[i], w_down[i], norm_ple[i], w_ple_gate[i], w_ple_proj[i])
        h = rms_norm(xp, norm_mix[i])
        q_a, k_a, v_a, q_b, k_b, v_b, logf = _mix_projections(
            h, w_in[i], b_f[i], q_norm_a[i], k_norm_a[i], q_norm_b[i], k_norm_b[i])
        o_a = chunk_band_attention_prompt(q_a, k_a, v_a, rel_bias_a[i])
        o_b = forgetting_attention_prompt(q_b, k_b, v_b, logf)
        xp = _layer_tail(xp, o_a, o_b, p_prompt[i], *tail_w)
        t_p = k_a.shape[1]
        keep = min(A_WINDOW, t_p)
        ka_p.append(k_a[:, t_p - keep:])
        va_p.append(v_a[:, t_p - keep:])
        kb_p.append(k_b)
        vb_p.append(v_b)
        lf_p.append(logf)
        h = rms_norm(xs, norm_mix[i])
        q_a, k_a, v_a, q_b, k_b, v_b, logf = _mix_projections(
            h, w_in[i], b_f[i], q_norm_a[i], k_norm_a[i], q_norm_b[i], k_norm_b[i])
        o_a = chunk_band_attention_sample(q_a, k_a, v_a, cache_k_a[i], cache_v_a[i], rel_bias_a[i])
        o_b = forgetting_attention_sample(q_b, k_b, v_b, logf, cache_k_b[i], cache_v_b[i], cache_logf_b[i])
        xs = _layer_tail(xs, o_a, o_b, p_sample[i], *tail_w)
        ka_s.append(k_a)
        va_s.append(v_a)
        kb_s.append(k_b)
        vb_s.append(v_b)
        lf_s.append(logf)
    return (xp, xs,
            jnp.stack(ka_p), jnp.stack(va_p), jnp.stack(kb_p), jnp.stack(vb_p), jnp.stack(lf_p),
            jnp.stack(ka_s), jnp.stack(va_s), jnp.stack(kb_s), jnp.stack(vb_s), jnp.stack(lf_s))
```

```python
import functools

import jax
import jax.numpy as jnp
from jax import lax
from jax.experimental import pallas as pl
from jax.experimental.pallas import tpu as pltpu

F32 = jnp.float32
BF16 = jnp.bfloat16

CHUNK = 64
HEAD_DIM = 64
A_LEFT_CHUNKS = 8
A_WINDOW = A_LEFT_CHUNKS * CHUNK
REL_CLIP = 128
ATTN_SCALE = HEAD_DIM ** -0.5
NEG_INF = -1e30
EPS = 1e-6

LANES = 128
HEADS_PER_GROUP = LANES // HEAD_DIM
VMEM_LIMIT = 56 * 1024 * 1024

ROW_TILE = 512
BAND_Q = 2 * CHUNK
BAND_W = A_WINDOW + BAND_Q
FOX_Q = 256
FOX_K = 512
FF_CHUNK = 256
SAMPLE_KEY_PAD = 128
FOX_S_KBLOCK = 1024


def _params(*sem):
    return pltpu.CompilerParams(dimension_semantics=sem, vmem_limit_bytes=VMEM_LIMIT)


def _const_spec(shape):
    zeros = (0,) * len(shape)
    return pl.BlockSpec(shape, lambda *_: zeros, pipeline_mode=pl.Buffered(1))


def _rms(x, gain):
    ms = jnp.mean(x * x, axis=-1, keepdims=True)
    return x * lax.rsqrt(ms + EPS) * gain


def _dot(a, b):
    return jnp.dot(a, b, preferred_element_type=F32)


def _dot_nt(a, b):
    return lax.dot_general(a, b, (((1,), (1,)), ((), ())), preferred_element_type=F32)


def _head_lane_mask(head, dtype):
    lane = lax.broadcasted_iota(jnp.int32, (1, LANES), 1)
    return ((lane // HEAD_DIM) == head).astype(dtype)


def _proj_kernel(x_ref, g_ref, w_ref, wf_ref, bf_ref, gains_ref, ones_ref,
                 qa_ref, qb_ref, ka_ref, va_ref, kb_ref, vb_ref,
                 ka16_ref, va16_ref, kb16_ref, vb16_ref, lf_ref, *, d_seg, n_fh):
    h = _rms(x_ref[...], g_ref[...]).astype(BF16)

    def seg(s):
        return _dot(h, w_ref[:, s * d_seg:(s + 1) * d_seg])

    def head_norm(y, gi):
        ss = _dot((y * y).astype(BF16), ones_ref[...])
        return y * lax.rsqrt(ss * (1.0 / HEAD_DIM) + EPS) * gains_ref[gi:gi + 1, :]

    qa_ref[...] = (head_norm(seg(0), 0) * ATTN_SCALE).astype(BF16)
    ka = head_norm(seg(1), 1)
    ka_ref[...] = ka
    ka16_ref[...] = ka.astype(BF16)
    va = seg(2)
    va_ref[...] = va
    va16_ref[...] = va.astype(BF16)
    qb_ref[...] = (head_norm(seg(3), 2) * ATTN_SCALE).astype(BF16)
    kb = head_norm(seg(4), 3)
    kb_ref[...] = kb
    kb16_ref[...] = kb.astype(BF16)
    vb = seg(5)
    vb_ref[...] = vb
    vb16_ref[...] = vb.astype(BF16)

    z = _dot(h, wf_ref[...]) + bf_ref[...]
    lf = jnp.minimum(z, 0.0) - jnp.log1p(jnp.exp(-jnp.abs(z)))
    lf_ref[...] = lf[:, :n_fh]


def _proj(x, g, w_qkv, w_f, b_f, gains, ones_bd, n_fh):
    n, d = x.shape
    d_seg = w_qkv.shape[1] // 6
    tm = min(ROW_TILE, n)
    assert n % tm == 0
    row = lambda width: pl.BlockSpec((tm, width), lambda i: (i, 0))
    f32_out = jax.ShapeDtypeStruct((n, d_seg), F32)
    bf_out = jax.ShapeDtypeStruct((n, d_seg), BF16)
    return pl.pallas_call(
        functools.partial(_proj_kernel, d_seg=d_seg, n_fh=n_fh),
        grid=(n // tm,),
        in_specs=[row(d), _const_spec(g.shape), _const_spec(w_qkv.shape), _const_spec(w_f.shape),
                  _const_spec(b_f.shape), _const_spec(gains.shape), _const_spec(ones_bd.shape)],
        out_specs=[row(d_seg)] * 10 + [row(n_fh)],
        out_shape=[bf_out, bf_out, f32_out, f32_out, f32_out, f32_out,
                   bf_out, bf_out, bf_out, bf_out, jax.ShapeDtypeStruct((n, n_fh), F32)],
        compiler_params=_params("parallel"),
        name="proj",
    )(x, g, w_qkv, w_f, b_f, gains, ones_bd)


def _cumsum_kernel(x_ref, u_ref, c_ref, *, n_blocks):
    u = u_ref[...]
    off = jnp.zeros((x_ref.shape[0], 1), F32)
    for j in range(n_blocks):
        x = x_ref[:, j * LANES:(j + 1) * LANES]
        hi = x.astype(BF16)
        r1 = x - hi.astype(F32)
        mid = r1.astype(BF16)
        lo = (r1 - mid.astype(F32)).astype(BF16)
        c = (_dot(hi, u) + _dot(mid, u) + _dot(lo, u)) + off
        c_ref[:, j * LANES:(j + 1) * LANES] = c
        off = c[:, LANES - 1:LANES]


def _cumsum_lanes(x):
    g, t = x.shape
    assert t % LANES == 0
    idx = jnp.arange(LANES)
    u = (idx[:, None] <= idx[None, :]).astype(BF16)
    return pl.pallas_call(
        functools.partial(_cumsum_kernel, n_blocks=t // LANES),
        out_shape=jax.ShapeDtypeStruct((g, t), F32),
        compiler_params=pltpu.CompilerParams(vmem_limit_bytes=VMEM_LIMIT),
        name="cumsum",
    )(x, u)


def _band_kernel(q_ref, k_ref, v_ref, bias_ref, o_ref, *, n_tiles):
    masks = [_head_lane_mask(h, BF16) for h in range(HEADS_PER_GROUP)]
    lane_head = lax.broadcasted_iota(jnp.int32, (1, LANES), 1) // HEAD_DIM

    def tile(q, kw, vw, col0):
        width = kw.shape[0]
        out = None
        for h in range(HEADS_PER_GROUP):
            s = _dot_nt(q * masks[h], kw) + bias_ref[0, h, :, col0:col0 + width]
            m = jnp.max(s, axis=-1, keepdims=True)
            p = jnp.exp(s - m)
            l = jnp.sum(p, axis=-1, keepdims=True)
            o = _dot(p.astype(BF16), vw) / l
            out = o if out is None else jnp.where(lane_head == h, o, out)
        return out.astype(o_ref.dtype)

    n_head_tiles = min(A_WINDOW // BAND_Q, n_tiles)
    for i in range(n_head_tiles):
        width = BAND_Q * (i + 1)
        q = q_ref[0, i * BAND_Q:(i + 1) * BAND_Q, :]
        o_ref[0, i * BAND_Q:(i + 1) * BAND_Q, :] = tile(q, k_ref[0, 0:width, :], v_ref[0, 0:width, :],
                                                         BAND_W - width)

    def body(i, carry):
        q0 = pl.multiple_of(i * BAND_Q, BAND_Q)
        k0 = pl.multiple_of(i * BAND_Q - A_WINDOW, BAND_Q)
        q = q_ref[0, pl.ds(q0, BAND_Q), :]
        o_ref[0, pl.ds(q0, BAND_Q), :] = tile(q, k_ref[0, pl.ds(k0, BAND_W), :], v_ref[0, pl.ds(k0, BAND_W), :], 0)
        return carry

    lax.fori_loop(n_head_tiles, n_tiles, body, 0)


def _band_prompt(q, k, v, bias):
    b, t, d = q.shape
    n_groups = d // LANES
    assert t % BAND_Q == 0
    seq = pl.BlockSpec((1, t, LANES), lambda i, j: (i, 0, j))
    return pl.pallas_call(
        functools.partial(_band_kernel, n_tiles=t // BAND_Q),
        grid=(b, n_groups),
        in_specs=[seq, seq, seq,
                  pl.BlockSpec((1, HEADS_PER_GROUP, BAND_Q, BAND_W), lambda i, j: (j, 0, 0, 0))],
        out_specs=seq,
        out_shape=jax.ShapeDtypeStruct((b, t, d), BF16),
        compiler_params=_params("parallel", "parallel"),
        name="band_prompt",
    )(q, k, v, bias)


def _softmax_step(carry, s, v):
    m, l, acc = carry
    m_new = jnp.maximum(m, jnp.max(s, axis=-1, keepdims=True))
    alpha = jnp.exp(m - m_new)
    p = jnp.exp(s - m_new)
    l = alpha * l + jnp.sum(p, axis=-1, keepdims=True)
    acc = alpha * acc + _dot(p.astype(BF16), v)
    return m_new, l, acc


def _softmax_init(rows, width):
    return (jnp.full((rows, 1), NEG_INF, F32), jnp.zeros((rows, 1), F32), jnp.zeros((rows, width), F32))


def _fox_kernel(q_ref, k_ref, v_ref, crow_ref, ccol_ref, o_ref, *, n_pairs):
    masks = [_head_lane_mask(h, BF16) for h in range(HEADS_PER_GROUP)]
    lane_head = lax.broadcasted_iota(jnp.int32, (1, LANES), 1) // HEAD_DIM
    causal = (lax.broadcasted_iota(jnp.int32, (FOX_Q, FOX_Q), 1)
              <= lax.broadcasted_iota(jnp.int32, (FOX_Q, FOX_Q), 0))

    def step(carry, qh, cq, h, k0, width, masked):
        kt = k_ref[0, pl.ds(k0, width), :]
        vt = v_ref[0, pl.ds(k0, width), :]
        kb = k0 // FOX_Q
        ck = jnp.concatenate([crow_ref[0, 0, kb + i, h:h + 1, :] for i in range(width // FOX_Q)], axis=1)
        s = _dot_nt(qh, kt) + (cq - ck)
        if masked:
            s = jnp.where(causal, s, NEG_INF)
        return _softmax_step(carry, s, vt)

    def q_tile(pair, second):
        k_diag = pl.multiple_of(pair * FOX_K, FOX_K)
        q0 = pl.multiple_of(k_diag + second * FOX_Q, FOX_Q)
        q = q_ref[0, pl.ds(q0, FOX_Q), :]
        out = None
        for h in range(HEADS_PER_GROUP):
            qh = q * masks[h]
            cq = ccol_ref[0, 0, pl.ds(q0, FOX_Q), h:h + 1]

            def full(j, carry):
                return step(carry, qh, cq, h, pl.multiple_of(j * FOX_K, FOX_K), FOX_K, False)

            carry = lax.fori_loop(0, pair, full, _softmax_init(FOX_Q, LANES))
            if second:
                carry = step(carry, qh, cq, h, k_diag, FOX_Q, False)
            m, l, acc = step(carry, qh, cq, h, q0, FOX_Q, True)
            o = acc / l
            out = o if out is None else jnp.where(lane_head == h, o, out)
        o_ref[0, pl.ds(q0, FOX_Q), :] = out.astype(o_ref.dtype)

    def body(pair, carry):
        q_tile(pair, 0)
        q_tile(pair, 1)
        return carry

    lax.fori_loop(0, n_pairs, body, 0)


def _fox_prompt(q, k, v, crow, ccol):
    b, t, d = q.shape
    n_groups = d // LANES
    assert t % FOX_K == 0 and FOX_K == 2 * FOX_Q
    seq = pl.BlockSpec((1, t, LANES), lambda i, j: (i, 0, j))
    return pl.pallas_call(
        functools.partial(_fox_kernel, n_pairs=t // FOX_K),
        grid=(b, n_groups),
        in_specs=[seq, seq, seq,
                  pl.BlockSpec((1, 1, t // FOX_Q, HEADS_PER_GROUP, FOX_Q), lambda i, j: (i, j, 0, 0, 0)),
                  pl.BlockSpec((1, 1, t, HEADS_PER_GROUP), lambda i, j: (i, j, 0, 0))],
        out_specs=seq,
        out_shape=jax.ShapeDtypeStruct((b, t, d), BF16),
        compiler_params=_params("parallel", "parallel"),
        name="fox_prompt",
    )(q, k, v, crow, ccol)


def _stack_heads(q, n_heads):
    lane_head = lax.broadcasted_iota(jnp.int32, (1, q.shape[1]), 1) // HEAD_DIM
    return jnp.concatenate([q * (lane_head == h).astype(q.dtype) for h in range(n_heads)], axis=0)


def _unstack_heads(o, n_heads):
    t = o.shape[0] // n_heads
    lane_head = lax.broadcasted_iota(jnp.int32, (1, o.shape[1]), 1) // HEAD_DIM
    out = o[0:t]
    for h in range(1, n_heads):
        out = jnp.where(lane_head == h, o[h * t:(h + 1) * t], out)
    return out


def _band_sample_kernel(q_ref, kc_ref, vc_ref, kn_ref, vn_ref, bias_ref, o_ref, *, n_heads):
    w = kc_ref.shape[1]
    qs = _stack_heads(q_ref[0], n_heads)
    s_c = _dot_nt(qs, kc_ref[0].astype(BF16)) + bias_ref[:, 0:w]
    s_n = _dot_nt(qs, kn_ref[0]) + bias_ref[:, w:w + SAMPLE_KEY_PAD]
    m = jnp.maximum(jnp.max(s_c, axis=-1, keepdims=True), jnp.max(s_n, axis=-1, keepdims=True))
    p_c = jnp.exp(s_c - m)
    p_n = jnp.exp(s_n - m)
    l = jnp.sum(p_c, axis=-1, keepdims=True) + jnp.sum(p_n, axis=-1, keepdims=True)
    o = (_dot(p_c.astype(BF16), vc_ref[0].astype(BF16)) + _dot(p_n.astype(BF16), vn_ref[0])) / l
    o_ref[0] = _unstack_heads(o, n_heads).astype(o_ref.dtype)


def _band_sample(q, k_cache, v_cache, k_new, v_new, bias):
    b, t, d = q.shape
    w = k_cache.shape[1]
    n_heads = d // HEAD_DIM
    per_b = lambda rows: pl.BlockSpec((1, rows, d), lambda i: (i, 0, 0))
    return pl.pallas_call(
        functools.partial(_band_sample_kernel, n_heads=n_heads),
        grid=(b,),
        in_specs=[per_b(t), per_b(w), per_b(w), per_b(SAMPLE_KEY_PAD), per_b(SAMPLE_KEY_PAD),
                  _const_spec(bias.shape)],
        out_specs=per_b(t),
        out_shape=jax.ShapeDtypeStruct((b, t, d), BF16),
        compiler_params=_params("parallel"),
        name="band_sample",
    )(q, k_cache, v_cache, k_new, v_new, bias)


def _fox_sample_kernel(q_ref, kc_ref, vc_ref, kn_ref, vn_ref, ckc_ref, ckn_ref, cq_ref, o_ref,
                       qs_ref, m_ref, l_ref, acc_ref, *, n_heads, t):
    j = pl.program_id(1)
    rows = n_heads * t

    @pl.when(j == 0)
    def _():
        qs_ref[...] = _stack_heads(q_ref[0], n_heads)
        m_ref[...] = jnp.full(m_ref.shape, NEG_INF, F32)
        l_ref[...] = jnp.zeros(l_ref.shape, F32)
        acc_ref[...] = jnp.zeros(acc_ref.shape, F32)

    def key_cumsum(ck_ref):
        width = ck_ref.shape[2]
        return jnp.concatenate([jnp.broadcast_to(ck_ref[0, h:h + 1, :], (t, width)) for h in range(n_heads)], axis=0)

    def update(s, v):
        m, l, acc = _softmax_step((m_ref[...], l_ref[...], acc_ref[...]), s, v)
        m_ref[...] = m
        l_ref[...] = l
        acc_ref[...] = acc

    cq = cq_ref[0]
    s = _dot_nt(qs_ref[...], kc_ref[0].astype(BF16)) + (cq - key_cumsum(ckc_ref))
    update(s, vc_ref[0].astype(BF16))

    @pl.when(j == pl.num_programs(1) - 1)
    def _():
        s_n = _dot_nt(qs_ref[...], kn_ref[0]) + (cq - key_cumsum(ckn_ref))
        q_pos = lax.broadcasted_iota(jnp.int32, (rows, SAMPLE_KEY_PAD), 0) % t
        k_pos = lax.broadcasted_iota(jnp.int32, (rows, SAMPLE_KEY_PAD), 1)
        update(jnp.where(k_pos <= q_pos, s_n, NEG_INF), vn_ref[0])
        o_ref[0] = _unstack_heads(acc_ref[...] / l_ref[...], n_heads).astype(o_ref.dtype)


def _fox_sample(q, k_cache, v_cache, k_new, v_new, c_keys, c_q):
    b, t, d = q.shape
    past = k_cache.shape[1]
    n_heads = d // HEAD_DIM
    kb = min(FOX_S_KBLOCK, past)
    assert past % kb == 0 and kb % SAMPLE_KEY_PAD == 0 and t <= SAMPLE_KEY_PAD
    rows = n_heads * t
    per_b = lambda r: pl.BlockSpec((1, r, d), lambda i, j: (i, 0, 0))
    cache = pl.BlockSpec((1, kb, d), lambda i, j: (i, j, 0))
    return pl.pallas_call(
        functools.partial(_fox_sample_kernel, n_heads=n_heads, t=t),
        grid=(b, past // kb),
        in_specs=[per_b(t), cache, cache, per_b(SAMPLE_KEY_PAD), per_b(SAMPLE_KEY_PAD),
                  pl.BlockSpec((1, n_heads, kb), lambda i, j: (i, 0, j)),
                  pl.BlockSpec((1, n_heads, SAMPLE_KEY_PAD), lambda i, j: (i, 0, past // SAMPLE_KEY_PAD)),
                  pl.BlockSpec((1, rows, 1), lambda i, j: (i, 0, 0))],
        out_specs=per_b(t),
        out_shape=jax.ShapeDtypeStruct((b, t, d), BF16),
        scratch_shapes=[pltpu.VMEM((rows, d), BF16), pltpu.VMEM((rows, 1), F32),
                        pltpu.VMEM((rows, 1), F32), pltpu.VMEM((rows, d), F32)],
        compiler_params=_params("parallel", "arbitrary"),
        name="fox_sample",
    )(q, k_cache, v_cache, k_new, v_new, c_keys, c_keys, c_q)


def _tail_kernel(x_ref, oa_ref, ob_ref, p_ref, woa_ref, wob_ref, nf_ref, wg_ref, wu_ref, wd_ref,
                 np_ref, wpg_ref, wpp_ref, y_ref, *, n_ff_chunks):
    x = x_ref[...] + _dot(oa_ref[...], woa_ref[...]) + _dot(ob_ref[...], wob_ref[...])
    h = _rms(x, nf_ref[...]).astype(BF16)
    ffn = None
    for c in range(n_ff_chunks):
        cols = slice(c * FF_CHUNK, (c + 1) * FF_CHUNK)
        g = _dot(h, wg_ref[:, cols])
        u = _dot(h, wu_ref[:, cols])
        part = _dot((g * jax.nn.sigmoid(g) * u).astype(BF16), wd_ref[cols, :])
        ffn = part if ffn is None else ffn + part
    x = x + ffn
    gate = jax.nn.sigmoid(_dot(_rms(x, np_ref[...]).astype(BF16), wpg_ref[...]))
    y_ref[...] = x + _dot(p_ref[...].astype(BF16), wpp_ref[...]) * gate


def _tail(x, oa, ob, p, woa, wob, nf, wg, wu, wd, npl, wpg, wpp):
    n, d = x.shape
    d_ff = wg.shape[1]
    tm = min(ROW_TILE, n)
    assert n % tm == 0 and d_ff % FF_CHUNK == 0
    row = lambda a: pl.BlockSpec((tm, a.shape[1]), lambda i: (i, 0))
    weights = (woa, wob, nf, wg, wu, wd, npl, wpg, wpp)
    return pl.pallas_call(
        functools.partial(_tail_kernel, n_ff_chunks=d_ff // FF_CHUNK),
        grid=(n // tm,),
        in_specs=[row(x), row(oa), row(ob), row(p)] + [_const_spec(w.shape) for w in weights],
        out_specs=row(x),
        out_shape=jax.ShapeDtypeStruct((n, d), F32),
        compiler_params=_params("parallel"),
        name="tail",
    )(x, oa, ob, p, *weights)


def _band_bias(table):
    r = jnp.arange(BAND_Q)[:, None]
    j = jnp.arange(BAND_W)[None, :]
    bias = table[:, jnp.clip(A_WINDOW + r - j, -REL_CLIP, REL_CLIP) + REL_CLIP].astype(F32)
    chunk_gap = j // CHUNK - r // CHUNK
    return bias, jnp.where((chunk_gap >= 0) & (chunk_gap <= A_LEFT_CHUNKS), bias, NEG_INF)


def _pad_rows(x, rows):
    return jnp.pad(x, ((0, 0), (0, rows - x.shape[1]), (0, 0)))


def kernel(x_prompt, x_sample, cache_k_a, cache_v_a, cache_k_b, cache_v_b, cache_logf_b, p_prompt, p_sample, norm_mix, w_in, b_f, q_norm_a, k_norm_a, q_norm_b, k_norm_b, rel_bias_a, w_out, norm_ffn, w_gate, w_up, w_down, norm_ple, w_ple_gate, w_ple_proj):
    depth = w_in.shape[0]
    b, t, d = x_prompt.shape
    sb, st, _ = x_sample.shape
    n_ha, n_hb = cache_k_a.shape[3], cache_k_b.shape[3]
    d_a, d_b = n_ha * HEAD_DIM, n_hb * HEAD_DIM
    past = cache_k_b.shape[2]
    w_a = cache_k_a.shape[2]
    assert d_a == d_b and w_a == A_WINDOW and st <= CHUNK and t >= A_WINDOW
    d_qkv = 3 * d_a + 3 * d_b
    n_groups = n_hb // HEADS_PER_GROUP

    head_idx = jnp.arange(d_a) // HEAD_DIM
    ones_bd = (head_idx[:, None] == head_idx[None, :]).astype(BF16)

    xp = x_prompt.reshape(b * t, d)
    xs = x_sample.reshape(sb * st, d)
    outs = [[] for _ in range(10)]
    for i in range(depth):
        w_qkv = w_in[i, :, :d_qkv].astype(BF16)
        w_f = jnp.pad(w_in[i, :, d_qkv:], ((0, 0), (0, LANES - n_hb))).astype(BF16)
        bf_row = jnp.pad(b_f[i], (0, LANES - n_hb)).reshape(1, LANES)
        gains = jnp.stack([jnp.tile(g[i], n_ha) for g in (q_norm_a, k_norm_a, q_norm_b, k_norm_b)])
        g_mix = norm_mix[i].reshape(1, d)
        tail_w = (w_out[i, :d_a].astype(BF16), w_out[i, d_a:].astype(BF16), norm_ffn[i].reshape(1, d),
                  w_gate[i].astype(BF16), w_up[i].astype(BF16), w_down[i].astype(BF16),
                  norm_ple[i].reshape(1, d), w_ple_gate[i].astype(BF16), w_ple_proj[i].astype(BF16))
        bias_free, bias_band = _band_bias(rel_bias_a[i])

        qa, qb, ka, va, kb, vb, ka16, va16, kb16, vb16, lf = _proj(xp, g_mix, w_qkv, w_f, bf_row, gains, ones_bd, n_hb)
        seq = lambda a: a.reshape(b, t, -1)
        c = _cumsum_lanes(seq(lf).transpose(0, 2, 1).reshape(b * n_hb, t)).reshape(b, n_groups, HEADS_PER_GROUP, t)
        o_a = _band_prompt(seq(qa), seq(ka16), seq(va16),
                           bias_band.reshape(n_ha // HEADS_PER_GROUP, HEADS_PER_GROUP, BAND_Q, BAND_W))
        c_rows = c.reshape(b, n_groups, HEADS_PER_GROUP, t // FOX_Q, FOX_Q).transpose(0, 1, 3, 2, 4)
        o_b = _fox_prompt(seq(qb), seq(kb16), seq(vb16), c_rows, c.transpose(0, 1, 3, 2))
        xp = _tail(xp, o_a.reshape(b * t, d_a), o_b.reshape(b * t, d_b), p_prompt[i].reshape(b * t, -1), *tail_w)
        heads = lambda a, n: a.reshape(b, t, n, HEAD_DIM)
        outs[0].append(heads(ka, n_ha)[:, t - A_WINDOW:])
        outs[1].append(heads(va, n_ha)[:, t - A_WINDOW:])
        outs[2].append(heads(kb, n_hb))
        outs[3].append(heads(vb, n_hb))
        outs[4].append(lf.reshape(b, t, n_hb))

        qa, qb, ka, va, kb, vb, ka16, va16, kb16, vb16, lf = _proj(xs, g_mix, w_qkv, w_f, bf_row, gains, ones_bd, n_hb)
        sseq = lambda a: a.reshape(sb, st, -1)
        pad = lambda a: _pad_rows(sseq(a), SAMPLE_KEY_PAD)
        lf_all = jnp.concatenate([cache_logf_b[i].astype(F32), sseq(lf)], axis=1)
        t_all = past + SAMPLE_KEY_PAD
        lf_all = _pad_rows(lf_all, t_all).transpose(0, 2, 1).reshape(sb * n_hb, t_all)
        c_all = _cumsum_lanes(lf_all).reshape(sb, n_hb, t_all)
        c_q = c_all[:, :, past:past + st].reshape(sb, n_hb * st, 1)
        bias_s = jnp.where(jnp.arange(BAND_W)[None, None, :] < w_a + st, bias_free[:, :st, :], NEG_INF)
        o_a = _band_sample(sseq(qa), cache_k_a[i].reshape(sb, w_a, d_a), cache_v_a[i].reshape(sb, w_a, d_a),
                           pad(ka16), pad(va16), bias_s.reshape(n_ha * st, BAND_W))
        o_b = _fox_sample(sseq(qb), cache_k_b[i].reshape(sb, past, d_b), cache_v_b[i].reshape(sb, past, d_b),
                          pad(kb16), pad(vb16), c_all, c_q)
        xs = _tail(xs, o_a.reshape(sb * st, d_a), o_b.reshape(sb * st, d_b), p_sample[i].reshape(sb * st, -1), *tail_w)
        sheads = lambda a, n: a.reshape(sb, st, n, HEAD_DIM)
        outs[5].append(sheads(ka, n_ha))
        outs[6].append(sheads(va, n_ha))
        outs[7].append(sheads(kb, n_hb))
        outs[8].append(sheads(vb, n_hb))
        outs[9].append(lf.reshape(sb, st, n_hb))

    return (xp.reshape(b, t, d), xs.reshape(sb, st, d)) + tuple(jnp.stack(o) for o in outs)
```

```python
import functools

import jax
import jax.numpy as jnp
from jax import lax
from jax.experimental import pallas as pl
from jax.experimental.pallas import tpu as pltpu

F32 = jnp.float32
BF16 = jnp.bfloat16

CHUNK = 64
HEAD_DIM = 64
A_LEFT_CHUNKS = 8
A_WINDOW = A_LEFT_CHUNKS * CHUNK
REL_CLIP = 128
ATTN_SCALE = HEAD_DIM ** -0.5
LOG2E = 1.4426950408889634
Q_SCALE = ATTN_SCALE * LOG2E
NEG_INF = -1e30
EPS = 1e-6

LANES = 128
HEADS_PER_GROUP = LANES // HEAD_DIM
VMEM_LIMIT = 56 * 1024 * 1024

ROW_TILE = 512
TIME_BLOCK = 256
BAND_Q = 4 * CHUNK
BAND_W = A_WINDOW + BAND_Q
FOX_T = 512
FF_CHUNK = 256
SAMPLE_KEY_PAD = 128
FOX_S_KBLOCK = 1024


def _params(*sem):
    return pltpu.CompilerParams(dimension_semantics=sem, vmem_limit_bytes=VMEM_LIMIT)


def _const_spec(shape):
    zeros = (0,) * len(shape)
    return pl.BlockSpec(shape, lambda *_: zeros, pipeline_mode=pl.Buffered(1))


def _rms(x, gain):
    ms = jnp.mean(x * x, axis=-1, keepdims=True)
    return x * lax.rsqrt(ms + EPS) * gain


def _dot(a, b):
    return jnp.dot(a, b, preferred_element_type=F32)


def _dot_nt(a, b):
    return lax.dot_general(a, b, (((1,), (1,)), ((), ())), preferred_element_type=F32)


def _proj_kernel(x_ref, g_ref, wt_ref, bf_ref, gains_ref,
                 qa_ref, qb_ref, ka16_ref, kb16_ref, va16_ref, vb16_ref,
                 ka_ref, va_ref, kb_ref, vb_ref, lf_ref, *, d_seg, n_fh, keep_from):
    h = _rms(x_ref[0], g_ref[...]).astype(BF16)
    n_heads = d_seg // HEAD_DIM
    n_slabs = qa_ref.shape[1]

    def seg(s, extra=0):
        return _dot_nt(wt_ref[s * d_seg:(s + 1) * d_seg + extra, :], h)

    def head_norm(y, gi):
        rows = []
        for hd in range(n_heads):
            yh = y[hd * HEAD_DIM:(hd + 1) * HEAD_DIM, :]
            ms = jnp.sum(yh * yh, axis=0, keepdims=True) * (1.0 / HEAD_DIM)
            rows.append(yh * lax.rsqrt(ms + EPS) * gains_ref[gi])
        return jnp.concatenate(rows, axis=0)

    def put_slabs(ref, y):
        y = y.astype(BF16)
        for u in range(n_slabs):
            ref[0, u] = y[:, u * TIME_BLOCK:(u + 1) * TIME_BLOCK]

    put_slabs(qa_ref, head_norm(seg(0), 0) * Q_SCALE)
    ka = head_norm(seg(1), 1)
    ka16_ref[0] = ka.T.astype(BF16)
    va = seg(2)
    put_slabs(va16_ref, va)

    @pl.when(pl.program_id(1) >= keep_from)
    def _():
        ka_ref[0] = ka
        va_ref[0] = va

    put_slabs(qb_ref, head_norm(seg(3), 2) * Q_SCALE)
    kb = head_norm(seg(4), 3)
    kb_ref[0] = kb
    kb16_ref[0] = kb.T.astype(BF16)
    tail = seg(5, n_fh)
    vb = tail[:d_seg]
    vb_ref[0] = vb
    put_slabs(vb16_ref, vb)
    z = tail[d_seg:] + bf_ref[...]
    lf_ref[0] = jnp.minimum(z, 0.0) - jnp.log1p(jnp.exp(-jnp.abs(z)))


def _proj(x, g, wt, bf_t, gains_t, n_fh, keep):
    b, t, d = x.shape
    d_seg = (wt.shape[0] - n_fh) // 6
    tm = ROW_TILE
    assert t % tm == 0 and keep % tm == 0 and tm % TIME_BLOCK == 0
    nt, n_slabs = t // tm, tm // TIME_BLOCK
    keep_from = nt - keep // tm
    feat = lambda rows: pl.BlockSpec((1, rows, tm), lambda i, j: (i, 0, j))
    slab = pl.BlockSpec((1, n_slabs, d_seg, TIME_BLOCK), lambda i, j: (i, j, 0, 0))
    kept = pl.BlockSpec((1, d_seg, tm), lambda i, j: (i, 0, jnp.maximum(j - keep_from, 0)))
    std = pl.BlockSpec((1, tm, d_seg), lambda i, j: (i, j, 0))
    slab_out = jax.ShapeDtypeStruct((b, t // TIME_BLOCK, d_seg, TIME_BLOCK), BF16)
    std_out = jax.ShapeDtypeStruct((b, t, d_seg), BF16)
    feat_out = jax.ShapeDtypeStruct((b, d_seg, t), F32)
    kept_out = jax.ShapeDtypeStruct((b, d_seg, keep), F32)
    return pl.pallas_call(
        functools.partial(_proj_kernel, d_seg=d_seg, n_fh=n_fh, keep_from=keep_from),
        grid=(b, nt),
        in_specs=[pl.BlockSpec((1, tm, d), lambda i, j: (i, j, 0)), _const_spec(g.shape), _const_spec(wt.shape),
                  _const_spec(bf_t.shape), _const_spec(gains_t.shape)],
        out_specs=[slab, slab, std, std, slab, slab, kept, kept, feat(d_seg), feat(d_seg), feat(n_fh)],
        out_shape=[slab_out, slab_out, std_out, std_out, slab_out, slab_out,
                   kept_out, kept_out, feat_out, feat_out, jax.ShapeDtypeStruct((b, n_fh, t), F32)],
        compiler_params=_params("parallel", "arbitrary"),
        name="proj",
    )(x, g, wt, bf_t, gains_t)


def _cumsum_kernel(x_ref, u_ref, c_ref, *, n_blocks):
    u = u_ref[...]
    off = jnp.zeros((x_ref.shape[0], 1), F32)
    for j in range(n_blocks):
        x = x_ref[:, j * LANES:(j + 1) * LANES]
        hi = x.astype(BF16)
        r1 = x - hi.astype(F32)
        mid = r1.astype(BF16)
        lo = (r1 - mid.astype(F32)).astype(BF16)
        c = (_dot(hi, u) + _dot(mid, u) + _dot(lo, u)) + off
        c_ref[:, j * LANES:(j + 1) * LANES] = c
        off = c[:, LANES - 1:LANES]


def _cumsum_lanes(x):
    g, t = x.shape
    assert t % LANES == 0
    idx = jnp.arange(LANES)
    u = (idx[:, None] <= idx[None, :]).astype(BF16)
    return pl.pallas_call(
        functools.partial(_cumsum_kernel, n_blocks=t // LANES),
        out_shape=jax.ShapeDtypeStruct((g, t), F32),
        compiler_params=pltpu.CompilerParams(vmem_limit_bytes=VMEM_LIMIT),
        name="cumsum",
    )(x, u)


def _pair_mask(tq):
    shape = (LANES, HEADS_PER_GROUP * tq)
    row_head = lax.broadcasted_iota(jnp.int32, shape, 0) // HEAD_DIM
    col_head = lax.broadcasted_iota(jnp.int32, shape, 1) // tq
    return (row_head == col_head).astype(F32).astype(BF16)


def _pair_diag(o2, tq):
    return jnp.concatenate([o2[h * HEAD_DIM:(h + 1) * HEAD_DIM, h * tq:(h + 1) * tq]
                            for h in range(HEADS_PER_GROUP)], axis=0)


def _v_dot_p(v_ref, blk0, p, rows=slice(None)):
    out = None
    for u in range(p.shape[0] // TIME_BLOCK):
        part = _dot(v_ref[0, blk0 + u, rows, :], p[u * TIME_BLOCK:(u + 1) * TIME_BLOCK, :])
        out = part if out is None else out + part
    return out


def _q_tile(q_ref, blk0, n_blk):
    parts = [q_ref[0, blk0 + u] for u in range(n_blk)]
    return parts[0] if n_blk == 1 else jnp.concatenate(parts, axis=1)


def _band_kernel(k_ref, qt_ref, vt_ref, bias_ref, o_ref, s_ref, *, n_tiles):
    head_mask = _pair_mask(BAND_Q)
    q_blks = BAND_Q // TIME_BLOCK

    def window(i):
        k0 = max(i * BAND_Q - A_WINDOW, 0)
        return k0, (i + 1) * BAND_Q - k0

    def scores(i):
        k0, width = window(i)
        q = _q_tile(qt_ref, i * q_blks, q_blks)
        q2 = jnp.concatenate([q] * HEADS_PER_GROUP, axis=1) * head_mask
        s_ref[i % 2, 0:width, :] = _dot(k_ref[0, k0:k0 + width, :], q2) + bias_ref[0, BAND_W - width:, :]

    def finish(i):
        k0, width = window(i)
        s = s_ref[i % 2, 0:width, :]
        p = jnp.exp2(s - jnp.max(s, axis=0, keepdims=True))
        l = jnp.sum(p, axis=0, keepdims=True)
        o2 = _v_dot_p(vt_ref, k0 // TIME_BLOCK, p.astype(BF16)) / l
        o_ref[0, i * BAND_Q:(i + 1) * BAND_Q, :] = _pair_diag(o2, BAND_Q).T.astype(o_ref.dtype)

    scores(0)
    for i in range(n_tiles):
        if i + 1 < n_tiles:
            scores(i + 1)
        finish(i)


def _band_prompt(k, q_t, v_t, bias):
    b, t, d = k.shape
    n_groups = d // LANES
    assert t % BAND_Q == 0 and BAND_Q % TIME_BLOCK == 0 and A_WINDOW % TIME_BLOCK == 0
    slab = pl.BlockSpec((1, t // TIME_BLOCK, LANES, TIME_BLOCK), lambda j, i: (i, 0, j, 0))
    std = pl.BlockSpec((1, t, LANES), lambda j, i: (i, 0, j))
    return pl.pallas_call(
        functools.partial(_band_kernel, n_tiles=t // BAND_Q),
        grid=(n_groups, b),
        in_specs=[std, slab, slab,
                  pl.BlockSpec((1, BAND_W, HEADS_PER_GROUP * BAND_Q), lambda j, i: (j, 0, 0))],
        out_specs=std,
        out_shape=jax.ShapeDtypeStruct((b, t, d), BF16),
        scratch_shapes=[pltpu.VMEM((2, BAND_W, HEADS_PER_GROUP * BAND_Q), F32)],
        compiler_params=_params("parallel", "parallel"),
        name="band_prompt",
    )(k, q_t, v_t, bias)


def _fox_kernel(k_ref, qt_ref, vt_ref, c_ref, o_ref, s_ref, *, n_tiles):
    head_mask = _pair_mask(FOX_T)
    blks = FOX_T // TIME_BLOCK
    wide = HEADS_PER_GROUP * FOX_T
    head_cols = [slice(h * FOX_T, (h + 1) * FOX_T) for h in range(HEADS_PER_GROUP)]
    head_rows = [slice(h * HEAD_DIM, (h + 1) * HEAD_DIM) for h in range(HEADS_PER_GROUP)]
    causal = (lax.broadcasted_iota(jnp.int32, (FOX_T, wide), 0)
              <= lax.broadcasted_iota(jnp.int32, (FOX_T, wide), 1) % FOX_T)

    q2_cache = {}

    def scores(n, i, j):
        if i not in q2_cache:
            q = _q_tile(qt_ref, i * blks, blks)
            q2_cache.clear()
            q2_cache[i] = jnp.concatenate([q] * HEADS_PER_GROUP, axis=1) * head_mask
        shift = (c_ref[0, 0, i * FOX_T:i * FOX_T + 1, :] - c_ref[0, 0, j * FOX_T:(j + 1) * FOX_T, :]) * LOG2E
        s = _dot(k_ref[0, j * FOX_T:(j + 1) * FOX_T, :], q2_cache[i]) + jnp.concatenate(
            [jnp.broadcast_to(shift[:, h:h + 1], (FOX_T, FOX_T)) for h in range(HEADS_PER_GROUP)], axis=1)
        if j == i:
            s = jnp.where(causal, s, NEG_INF)
        s_ref[n % 2] = s

    def update(n, j, carry):
        m, l, accs = carry
        s = s_ref[n % 2]
        m_new = jnp.maximum(m, jnp.max(s, axis=0, keepdims=True))
        alpha = jnp.exp2(m - m_new)
        p = jnp.exp2(s - m_new)
        l = alpha * l + jnp.sum(p, axis=0, keepdims=True)
        p = p.astype(BF16)
        accs = tuple(alpha[:, head_cols[h]] * accs[h]
                     + _v_dot_p(vt_ref, j * blks, p[:, head_cols[h]], head_rows[h])
                     for h in range(HEADS_PER_GROUP))
        return m_new, l, accs

    pairs = [(i, j) for i in range(n_tiles) for j in range(i + 1)]
    scores(0, *pairs[0])
    carry = None
    for n, (i, j) in enumerate(pairs):
        if n + 1 < len(pairs):
            scores(n + 1, *pairs[n + 1])
        if j == 0:
            carry = (jnp.full((1, wide), NEG_INF, F32), jnp.zeros((1, wide), F32),
                     tuple(jnp.zeros((HEAD_DIM, FOX_T), F32) for _ in range(HEADS_PER_GROUP)))
        carry = update(n, j, carry)
        if j == i:
            _, l, accs = carry
            out = jnp.concatenate([accs[h] / l[:, head_cols[h]] for h in range(HEADS_PER_GROUP)], axis=0)
            o_ref[0, i * FOX_T:(i + 1) * FOX_T, :] = out.T.astype(o_ref.dtype)


def _fox_prompt(k, q_t, v_t, c_cols):
    b, t, d = k.shape
    n_groups = d // LANES
    assert t % FOX_T == 0 and FOX_T % TIME_BLOCK == 0
    slab = pl.BlockSpec((1, t // TIME_BLOCK, LANES, TIME_BLOCK), lambda i, j: (i, 0, j, 0))
    std = pl.BlockSpec((1, t, LANES), lambda i, j: (i, 0, j))
    return pl.pallas_call(
        functools.partial(_fox_kernel, n_tiles=t // FOX_T),
        grid=(b, n_groups),
        in_specs=[std, slab, slab,
                  pl.BlockSpec((1, 1, t, HEADS_PER_GROUP), lambda i, j: (i, j, 0, 0))],
        out_specs=std,
        out_shape=jax.ShapeDtypeStruct((b, t, d), BF16),
        scratch_shapes=[pltpu.VMEM((2, FOX_T, HEADS_PER_GROUP * FOX_T), F32)],
        compiler_params=_params("parallel", "parallel"),
        name="fox_prompt",
    )(k, q_t, v_t, c_cols)


def _stack_heads(q, n_heads):
    lane_head = lax.broadcasted_iota(jnp.int32, (1, q.shape[1]), 1) // HEAD_DIM
    return jnp.concatenate([q * (lane_head == h).astype(q.dtype) for h in range(n_heads)], axis=0)


def _unstack_heads(o, n_heads):
    t = o.shape[0] // n_heads
    lane_head = lax.broadcasted_iota(jnp.int32, (1, o.shape[1]), 1) // HEAD_DIM
    out = o[0:t]
    for h in range(1, n_heads):
        out = jnp.where(lane_head == h, o[h * t:(h + 1) * t], out)
    return out


def _softmax_step(carry, s, v_t):
    m, l, acc = carry
    m_new = jnp.maximum(m, jnp.max(s, axis=-1, keepdims=True))
    alpha = jnp.exp2(m - m_new)
    p = jnp.exp2(s - m_new)
    l = alpha * l + jnp.sum(p, axis=-1, keepdims=True)
    acc = alpha * acc + _dot_nt(p.astype(BF16), v_t)
    return m_new, l, acc


def _band_sample_kernel(q_ref, kc_ref, vc_ref, kn_ref, vn_ref, bias_ref, o_ref, *, n_heads):
    w = kc_ref.shape[2]
    qs = _stack_heads(q_ref[0], n_heads)
    s_c = _dot(qs, kc_ref[0].astype(BF16)) + bias_ref[:, 0:w]
    s_n = _dot(qs, kn_ref[0]) + bias_ref[:, w:w + SAMPLE_KEY_PAD]
    m = jnp.maximum(jnp.max(s_c, axis=-1, keepdims=True), jnp.max(s_n, axis=-1, keepdims=True))
    p_c = jnp.exp2(s_c - m)
    p_n = jnp.exp2(s_n - m)
    l = jnp.sum(p_c, axis=-1, keepdims=True) + jnp.sum(p_n, axis=-1, keepdims=True)
    o = (_dot_nt(p_c.astype(BF16), vc_ref[0].astype(BF16)) + _dot_nt(p_n.astype(BF16), vn_ref[0])) / l
    o_ref[0] = _unstack_heads(o, n_heads).astype(o_ref.dtype)


def _band_sample(q, k_cache_t, v_cache_t, k_new_t, v_new_t, bias):
    b, t, d = q.shape
    w = k_cache_t.shape[2]
    n_heads = d // HEAD_DIM
    feat = lambda cols: pl.BlockSpec((1, d, cols), lambda i: (i, 0, 0))
    return pl.pallas_call(
        functools.partial(_band_sample_kernel, n_heads=n_heads),
        grid=(b,),
        in_specs=[pl.BlockSpec((1, t, d), lambda i: (i, 0, 0)), feat(w), feat(w),
                  feat(SAMPLE_KEY_PAD), feat(SAMPLE_KEY_PAD), _const_spec(bias.shape)],
        out_specs=pl.BlockSpec((1, t, d), lambda i: (i, 0, 0)),
        out_shape=jax.ShapeDtypeStruct((b, t, d), BF16),
        compiler_params=_params("parallel"),
        name="band_sample",
    )(q, k_cache_t, v_cache_t, k_new_t, v_new_t, bias)


def _fox_sample_kernel(q_ref, kc_ref, vc_ref, kn_ref, vn_ref, ckc_ref, ckn_ref, cq_ref, o_ref,
                       qs_ref, m_ref, l_ref, acc_ref, *, n_heads, t):
    j = pl.program_id(1)
    rows = n_heads * t

    @pl.when(j == 0)
    def _():
        qs_ref[...] = _stack_heads(q_ref[0], n_heads)
        m_ref[...] = jnp.full(m_ref.shape, NEG_INF, F32)
        l_ref[...] = jnp.zeros(l_ref.shape, F32)
        acc_ref[...] = jnp.zeros(acc_ref.shape, F32)

    def key_cumsum(ck_ref):
        width = ck_ref.shape[2]
        return jnp.concatenate([jnp.broadcast_to(ck_ref[0, h:h + 1, :], (t, width)) for h in range(n_heads)], axis=0)

    def update(s, v_t):
        m, l, acc = _softmax_step((m_ref[...], l_ref[...], acc_ref[...]), s, v_t)
        m_ref[...] = m
        l_ref[...] = l
        acc_ref[...] = acc

    cq = cq_ref[0]
    s = _dot(qs_ref[...], kc_ref[0].astype(BF16)) + (cq - key_cumsum(ckc_ref)) * LOG2E
    update(s, vc_ref[0].astype(BF16))

    @pl.when(j == pl.num_programs(1) - 1)
    def _():
        s_n = _dot(qs_ref[...], kn_ref[0]) + (cq - key_cumsum(ckn_ref)) * LOG2E
        q_pos = lax.broadcasted_iota(jnp.int32, (rows, SAMPLE_KEY_PAD), 0) % t
        k_pos = lax.broadcasted_iota(jnp.int32, (rows, SAMPLE_KEY_PAD), 1)
        update(jnp.where(k_pos <= q_pos, s_n, NEG_INF), vn_ref[0])
        o_ref[0] = _unstack_heads(acc_ref[...] / l_ref[...], n_heads).astype(o_ref.dtype)


def _fox_sample(q, k_cache_t, v_cache_t, k_new_t, v_new_t, c_keys, c_q):
    b, t, d = q.shape
    past = k_cache_t.shape[2]
    n_heads = d // HEAD_DIM
    kb = min(FOX_S_KBLOCK, past)
    assert past % kb == 0 and kb % SAMPLE_KEY_PAD == 0 and t <= SAMPLE_KEY_PAD
    rows = n_heads * t
    per_b = lambda shape: pl.BlockSpec((1,) + shape, lambda i, j: (i, 0, 0))
    cache = pl.BlockSpec((1, d, kb), lambda i, j: (i, 0, j))
    return pl.pallas_call(
        functools.partial(_fox_sample_kernel, n_heads=n_heads, t=t),
        grid=(b, past // kb),
        in_specs=[per_b((t, d)), cache, cache, per_b((d, SAMPLE_KEY_PAD)), per_b((d, SAMPLE_KEY_PAD)),
                  pl.BlockSpec((1, n_heads, kb), lambda i, j: (i, 0, j)),
                  pl.BlockSpec((1, n_heads, SAMPLE_KEY_PAD), lambda i, j: (i, 0, past // SAMPLE_KEY_PAD)),
                  per_b((rows, 1))],
        out_specs=per_b((t, d)),
        out_shape=jax.ShapeDtypeStruct((b, t, d), BF16),
        scratch_shapes=[pltpu.VMEM((rows, d), BF16), pltpu.VMEM((rows, 1), F32),
                        pltpu.VMEM((rows, 1), F32), pltpu.VMEM((rows, d), F32)],
        compiler_params=_params("parallel", "arbitrary"),
        name="fox_sample",
    )(q, k_cache_t, v_cache_t, k_new_t, v_new_t, c_keys, c_keys, c_q)


def _tail_kernel(x_ref, oa_ref, ob_ref, p_ref, woa_ref, wob_ref, nf_ref, wg_ref, wu_ref, wd_ref,
                 np_ref, wpg_ref, wpp_ref, y_ref, *, n_ff_chunks):
    x = x_ref[...] + _dot(oa_ref[...], woa_ref[...]) + _dot(ob_ref[...], wob_ref[...])
    h = _rms(x, nf_ref[...]).astype(BF16)
    ffn = None
    for c in range(n_ff_chunks):
        cols = slice(c * FF_CHUNK, (c + 1) * FF_CHUNK)
        g = _dot(h, wg_ref[:, cols])
        u = _dot(h, wu_ref[:, cols])
        part = _dot((g * jax.nn.sigmoid(g) * u).astype(BF16), wd_ref[cols, :])
        ffn = part if ffn is None else ffn + part
    x = x + ffn
    gate = jax.nn.sigmoid(_dot(_rms(x, np_ref[...]).astype(BF16), wpg_ref[...]))
    y_ref[...] = x + _dot(p_ref[...].astype(BF16), wpp_ref[...]) * gate


def _tail(x, oa, ob, p, woa, wob, nf, wg, wu, wd, npl, wpg, wpp):
    n, d = x.shape
    d_ff = wg.shape[1]
    tm = min(ROW_TILE, n)
    assert n % tm == 0 and d_ff % FF_CHUNK == 0
    row = lambda a: pl.BlockSpec((tm, a.shape[1]), lambda i: (i, 0))
    weights = (woa, wob, nf, wg, wu, wd, npl, wpg, wpp)
    return pl.pallas_call(
        functools.partial(_tail_kernel, n_ff_chunks=d_ff // FF_CHUNK),
        grid=(n // tm,),
        in_specs=[row(x), row(oa), row(ob), row(p)] + [_const_spec(w.shape) for w in weights],
        out_specs=row(x),
        out_shape=jax.ShapeDtypeStruct((n, d), F32),
        compiler_params=_params("parallel"),
        name="tail",
    )(x, oa, ob, p, *weights)


def _band_bias(table):
    table = table.astype(F32) * LOG2E
    n = BAND_W + BAND_Q - 1
    edge = BAND_W - 1 - A_WINDOW - REL_CLIP
    gen = jnp.concatenate([jnp.repeat(table[:, :1], edge, axis=1), table,
                           jnp.repeat(table[:, -1:], n - edge - table.shape[1], axis=1),
                           jnp.zeros((table.shape[0], 1), F32)], axis=1)
    flat = jnp.tile(gen, (1, BAND_W))[:, BAND_W - 1:BAND_W - 1 + BAND_W * n]
    bias = flat.reshape(table.shape[0], BAND_W, n)[:, :, :BAND_Q]
    j = jnp.arange(BAND_W)[:, None]
    r = jnp.arange(BAND_Q)[None, :]
    chunk_gap = j // CHUNK - r // CHUNK
    return bias, jnp.where((chunk_gap >= 0) & (chunk_gap <= A_LEFT_CHUNKS), bias, NEG_INF)


def _feature_major(cache):
    b, t, n, hd = cache.shape
    return cache.transpose(0, 2, 3, 1).reshape(b, n * hd, t)


def _time_major(x_t, n_heads):
    b, _, t = x_t.shape
    return x_t.reshape(b, n_heads, HEAD_DIM, t).transpose(0, 3, 1, 2)


def kernel(x_prompt, x_sample, cache_k_a, cache_v_a, cache_k_b, cache_v_b, cache_logf_b, p_prompt, p_sample, norm_mix, w_in, b_f, q_norm_a, k_norm_a, q_norm_b, k_norm_b, rel_bias_a, w_out, norm_ffn, w_gate, w_up, w_down, norm_ple, w_ple_gate, w_ple_proj):
    depth = w_in.shape[0]
    b, t, d = x_prompt.shape
    sb, st, _ = x_sample.shape
    n_ha, n_hb = cache_k_a.shape[3], cache_k_b.shape[3]
    d_a, d_b = n_ha * HEAD_DIM, n_hb * HEAD_DIM
    past = cache_k_b.shape[2]
    w_a = cache_k_a.shape[2]
    n_s = sb * st
    assert d_a == d_b and w_a == A_WINDOW and st <= CHUNK and t >= A_WINDOW and n_s % ROW_TILE == 0
    n_groups = n_hb // HEADS_PER_GROUP

    xp = x_prompt
    xs = x_sample.reshape(1, n_s, d)
    outs = [[] for _ in range(10)]
    for i in range(depth):
        wt = w_in[i].T.astype(BF16)
        bf_t = jnp.broadcast_to(b_f[i][:, None], (n_hb, ROW_TILE)).astype(F32)
        gains_t = jnp.stack([jnp.broadcast_to(g[i][:, None], (HEAD_DIM, ROW_TILE))
                             for g in (q_norm_a, k_norm_a, q_norm_b, k_norm_b)]).astype(F32)
        g_mix = norm_mix[i].reshape(1, d)
        tail_w = (w_out[i, :d_a].astype(BF16), w_out[i, d_a:].astype(BF16), norm_ffn[i].reshape(1, d),
                  w_gate[i].astype(BF16), w_up[i].astype(BF16), w_down[i].astype(BF16),
                  norm_ple[i].reshape(1, d), w_ple_gate[i].astype(BF16), w_ple_proj[i].astype(BF16))
        bias_free, bias_band = _band_bias(rel_bias_a[i])

        qa_t, qb_t, ka16, kb16, va_t16, vb_t16, ka_t, va_t, kb_t, vb_t, lf_t = _proj(
            xp, g_mix, wt, bf_t, gains_t, n_hb, A_WINDOW)
        c = _cumsum_lanes(lf_t.reshape(b * n_hb, t)).reshape(b, n_groups, HEADS_PER_GROUP, t)
        n_ga = n_ha // HEADS_PER_GROUP
        bias_pairs = (bias_band.reshape(n_ga, HEADS_PER_GROUP, BAND_W, BAND_Q).transpose(0, 2, 1, 3)
                      .reshape(n_ga, BAND_W, HEADS_PER_GROUP * BAND_Q))
        o_a = _band_prompt(ka16, qa_t, va_t16, bias_pairs)
        o_b = _fox_prompt(kb16, qb_t, vb_t16, c.transpose(0, 1, 3, 2))
        xp = _tail(xp.reshape(b * t, d), o_a.reshape(b * t, d_a), o_b.reshape(b * t, d_b),
                   p_prompt[i].reshape(b * t, -1), *tail_w).reshape(b, t, d)
        outs[0].append(_time_major(ka_t, n_ha))
        outs[1].append(_time_major(va_t, n_ha))
        outs[2].append(_time_major(kb_t, n_hb))
        outs[3].append(_time_major(vb_t, n_hb))
        outs[4].append(lf_t.transpose(0, 2, 1))

        qa_t, qb_t, ka16, kb16, va_t16, vb_t16, ka_t, va_t, kb_t, vb_t, lf_t = _proj(
            xs, g_mix, wt, bf_t, gains_t, n_hb, n_s)
        per_stream = lambda x_t: x_t.reshape(-1, sb, st).transpose(1, 0, 2)
        unslab = lambda s_t: s_t[0].transpose(1, 0, 2).reshape(s_t.shape[2], n_s)
        new_keys = lambda x_t: jnp.pad(per_stream(x_t).astype(BF16), ((0, 0), (0, 0), (0, SAMPLE_KEY_PAD - st)))
        q_rows = lambda s_t: unslab(s_t).T.reshape(sb, st, -1)
        lf_s = per_stream(lf_t[0])
        lf_all = jnp.concatenate([cache_logf_b[i].astype(F32).transpose(0, 2, 1), lf_s], axis=2)
        t_all = past + SAMPLE_KEY_PAD
        lf_all = jnp.pad(lf_all, ((0, 0), (0, 0), (0, t_all - past - st))).reshape(sb * n_hb, t_all)
        c_all = _cumsum_lanes(lf_all).reshape(sb, n_hb, t_all)
        c_q = c_all[:, :, past:past + st].reshape(sb, n_hb * st, 1)
        bias_s = jnp.where(jnp.arange(BAND_W)[None, None, :] < w_a + st,
                           bias_free[:, :, :st].transpose(0, 2, 1), NEG_INF)
        o_a = _band_sample(q_rows(qa_t), _feature_major(cache_k_a[i]), _feature_major(cache_v_a[i]),
                           new_keys(ka_t[0]), new_keys(va_t[0]), bias_s.reshape(n_ha * st, BAND_W))
        o_b = _fox_sample(q_rows(qb_t), _feature_major(cache_k_b[i]), _feature_major(cache_v_b[i]),
                          new_keys(kb_t[0]), new_keys(vb_t[0]), c_all, c_q)
        xs = _tail(xs.reshape(n_s, d), o_a.reshape(n_s, d_a), o_b.reshape(n_s, d_b),
                   p_sample[i].reshape(n_s, -1), *tail_w).reshape(1, n_s, d)
        outs[5].append(_time_major(per_stream(ka_t[0]), n_ha))
        outs[6].append(_time_major(per_stream(va_t[0]), n_ha))
        outs[7].append(_time_major(per_stream(kb_t[0]), n_hb))
        outs[8].append(_time_major(per_stream(vb_t[0]), n_hb))
        outs[9].append(lf_s.transpose(0, 2, 1))

    return (xp, xs.reshape(sb, st, d)) + tuple(jnp.stack(o) for o in outs)
```

```python
import functools

import jax
import jax.numpy as jnp
from jax import lax
from jax.experimental import pallas as pl
from jax.experimental.pallas import tpu as pltpu

F32 = jnp.float32
BF16 = jnp.bfloat16

CHUNK = 64
HEAD_DIM = 64
A_LEFT_CHUNKS = 8
A_WINDOW = A_LEFT_CHUNKS * CHUNK
REL_CLIP = 128
ATTN_SCALE = HEAD_DIM ** -0.5
LOG2E = 1.4426950408889634
Q_SCALE = ATTN_SCALE * LOG2E
NEG_INF = -1e30
EPS = 1e-6

LANES = 128
HEADS_PER_GROUP = LANES // HEAD_DIM
VMEM_LIMIT = 56 * 1024 * 1024

ROW_TILE = 512
TIME_BLOCK = 256
BAND_Q = 4 * CHUNK
BAND_W = A_WINDOW + BAND_Q
FOX_T = 512
FF_CHUNK = 256
ONES_ROWS = 16
BIAS_ROWS = 16
N_SPLIT = 3
SAMPLE_KEY_PAD = 128
FOX_S_KBLOCK = 1024


def _params(*sem):
    return pltpu.CompilerParams(dimension_semantics=sem, vmem_limit_bytes=VMEM_LIMIT)


def _const_spec(shape):
    zeros = (0,) * len(shape)
    return pl.BlockSpec(shape, lambda *_: zeros, pipeline_mode=pl.Buffered(1))


def _rms(x, gain):
    ms = jnp.mean(x * x, axis=-1, keepdims=True)
    return x * lax.rsqrt(ms + EPS) * gain


def _dot(a, b):
    return jnp.dot(a, b, preferred_element_type=F32)


def _dot_nt(a, b):
    return lax.dot_general(a, b, (((1,), (1,)), ((), ())), preferred_element_type=F32)


def _proj_kernel(x_ref, g_ref, wt_ref, bf_ref, gains_ref,
                 qa_ref, qb_ref, ka16_ref, kb16_ref, va16_ref, vb16_ref,
                 ka_ref, va_ref, kb_ref, vb_ref, lf_ref, *, d_seg, n_fh, keep_from):
    h = _rms(x_ref[0], g_ref[...]).astype(BF16)
    n_heads = d_seg // HEAD_DIM
    n_slabs = qa_ref.shape[1]

    def seg(s, extra=0):
        return _dot_nt(wt_ref[s * d_seg:(s + 1) * d_seg + extra, :], h)

    def head_norm(y, gi):
        rows = []
        for hd in range(n_heads):
            yh = y[hd * HEAD_DIM:(hd + 1) * HEAD_DIM, :]
            ms = jnp.sum(yh * yh, axis=0, keepdims=True) * (1.0 / HEAD_DIM)
            rows.append(yh * lax.rsqrt(ms + EPS) * gains_ref[gi])
        return jnp.concatenate(rows, axis=0)

    def put_slabs(ref, y):
        y = y.astype(BF16)
        for u in range(n_slabs):
            ref[0, u] = y[:, u * TIME_BLOCK:(u + 1) * TIME_BLOCK]

    put_slabs(qa_ref, head_norm(seg(0), 0) * Q_SCALE)
    ka = head_norm(seg(1), 1)
    ka16_ref[0] = ka.T.astype(BF16)
    va = seg(2)
    put_slabs(va16_ref, va)

    @pl.when(pl.program_id(1) >= keep_from)
    def _():
        ka_ref[0] = ka
        va_ref[0] = va

    put_slabs(qb_ref, head_norm(seg(3), 2) * Q_SCALE)
    kb = head_norm(seg(4), 3)
    kb_ref[0] = kb
    kb16_ref[0] = kb.T.astype(BF16)
    tail = seg(5, n_fh)
    vb = tail[:d_seg]
    vb_ref[0] = vb
    put_slabs(vb16_ref, vb)
    z = tail[d_seg:] + bf_ref[...]
    lf_ref[0] = jnp.minimum(z, 0.0) - jnp.log1p(jnp.exp(-jnp.abs(z)))


def _proj(x, g, wt, bf_t, gains_t, n_fh, keep):
    b, t, d = x.shape
    d_seg = (wt.shape[0] - n_fh) // 6
    tm = ROW_TILE
    assert t % tm == 0 and keep % tm == 0 and tm % TIME_BLOCK == 0
    nt, n_slabs = t // tm, tm // TIME_BLOCK
    keep_from = nt - keep // tm
    feat = lambda rows: pl.BlockSpec((1, rows, tm), lambda i, j: (i, 0, j))
    slab = pl.BlockSpec((1, n_slabs, d_seg, TIME_BLOCK), lambda i, j: (i, j, 0, 0))
    kept = pl.BlockSpec((1, d_seg, tm), lambda i, j: (i, 0, jnp.maximum(j - keep_from, 0)))
    std = pl.BlockSpec((1, tm, d_seg), lambda i, j: (i, j, 0))
    slab_out = jax.ShapeDtypeStruct((b, t // TIME_BLOCK, d_seg, TIME_BLOCK), BF16)
    std_out = jax.ShapeDtypeStruct((b, t, d_seg), BF16)
    feat_out = jax.ShapeDtypeStruct((b, d_seg, t), F32)
    kept_out = jax.ShapeDtypeStruct((b, d_seg, keep), F32)
    return pl.pallas_call(
        functools.partial(_proj_kernel, d_seg=d_seg, n_fh=n_fh, keep_from=keep_from),
        grid=(b, nt),
        in_specs=[pl.BlockSpec((1, tm, d), lambda i, j: (i, j, 0)), _const_spec(g.shape), _const_spec(wt.shape),
                  _const_spec(bf_t.shape), _const_spec(gains_t.shape)],
        out_specs=[slab, slab, std, std, slab, slab, kept, kept, feat(d_seg), feat(d_seg), feat(n_fh)],
        out_shape=[slab_out, slab_out, std_out, std_out, slab_out, slab_out,
                   kept_out, kept_out, feat_out, feat_out, jax.ShapeDtypeStruct((b, n_fh, t), F32)],
        compiler_params=_params("parallel", "arbitrary"),
        name="proj",
    )(x, g, wt, bf_t, gains_t)


def _cumsum_kernel(x_ref, u_ref, c_ref, *, n_blocks):
    u = u_ref[...]
    off = jnp.zeros((x_ref.shape[0], 1), F32)
    for j in range(n_blocks):
        x = x_ref[:, j * LANES:(j + 1) * LANES]
        hi = x.astype(BF16)
        r1 = x - hi.astype(F32)
        mid = r1.astype(BF16)
        lo = (r1 - mid.astype(F32)).astype(BF16)
        c = (_dot(hi, u) + _dot(mid, u) + _dot(lo, u)) + off
        c_ref[:, j * LANES:(j + 1) * LANES] = c
        off = c[:, LANES - 1:LANES]


def _cumsum_lanes(x):
    g, t = x.shape
    assert t % LANES == 0
    idx = jnp.arange(LANES)
    u = (idx[:, None] <= idx[None, :]).astype(BF16)
    return pl.pallas_call(
        functools.partial(_cumsum_kernel, n_blocks=t // LANES),
        out_shape=jax.ShapeDtypeStruct((g, t), F32),
        compiler_params=pltpu.CompilerParams(vmem_limit_bytes=VMEM_LIMIT),
        name="cumsum",
    )(x, u)


def _pair_mask(tq):
    shape = (LANES, HEADS_PER_GROUP * tq)
    row_head = lax.broadcasted_iota(jnp.int32, shape, 0) // HEAD_DIM
    col_head = lax.broadcasted_iota(jnp.int32, shape, 1) // tq
    return (row_head == col_head).astype(F32).astype(BF16)


def _pair_diag(o2, tq):
    return jnp.concatenate([o2[h * HEAD_DIM:(h + 1) * HEAD_DIM, h * tq:(h + 1) * tq]
                            for h in range(HEADS_PER_GROUP)], axis=0)


def _v_dot_p(v_ref, blk0, p, rows=slice(None)):
    ones = jnp.ones((ONES_ROWS, TIME_BLOCK), BF16)
    out = None
    for u in range(p.shape[0] // TIME_BLOCK):
        v_ext = jnp.concatenate([v_ref[0, blk0 + u, rows, :], ones], axis=0)
        part = _dot(v_ext, p[u * TIME_BLOCK:(u + 1) * TIME_BLOCK, :])
        out = part if out is None else out + part
    return out


def _q_tile(q_ref, blk0, n_blk):
    parts = [q_ref[0, blk0 + u] for u in range(n_blk)]
    return parts[0] if n_blk == 1 else jnp.concatenate(parts, axis=1)


def _band_kernel(k_ref, qt_ref, vt_ref, gen_ref, o_ref, s_ref, bias_ref, *, n_tiles):
    head_mask = _pair_mask(BAND_Q)
    q_blks = BAND_Q // TIME_BLOCK
    period = gen_ref.shape[2]

    @pl.when(pl.program_id(1) == 0)
    def _():
        j = lax.broadcasted_iota(jnp.int32, (BAND_W, BAND_Q), 0)
        r = lax.broadcasted_iota(jnp.int32, (BAND_W, BAND_Q), 1)
        gap = j // CHUNK - r // CHUNK
        in_band = (gap >= 0) & (gap <= A_LEFT_CHUNKS)
        for h in range(HEADS_PER_GROUP):
            rows = jnp.broadcast_to(gen_ref[0, h:h + 1, :], (BAND_W, period))
            toeplitz = pltpu.roll(rows, period - BAND_W + 1, 1, stride=1, stride_axis=0)[:, :BAND_Q]
            bias_ref[:, h * BAND_Q:(h + 1) * BAND_Q] = jnp.where(in_band, toeplitz, NEG_INF)

    def window(i):
        k0 = max(i * BAND_Q - A_WINDOW, 0)
        return k0, (i + 1) * BAND_Q - k0

    def scores(i):
        k0, width = window(i)
        q = _q_tile(qt_ref, i * q_blks, q_blks)
        q2 = jnp.concatenate([q] * HEADS_PER_GROUP, axis=1) * head_mask
        s_ref[i % 2, 0:width, :] = _dot(k_ref[0, k0:k0 + width, :], q2) + bias_ref[BAND_W - width:, :]

    def finish(i):
        k0, width = window(i)
        s = s_ref[i % 2, 0:width, :]
        p = jnp.exp2(s - jnp.max(s, axis=0, keepdims=True))
        o2 = _v_dot_p(vt_ref, k0 // TIME_BLOCK, p.astype(BF16))
        o2 = o2[:LANES] / o2[LANES:LANES + 1]
        o_ref[0, i * BAND_Q:(i + 1) * BAND_Q, :] = _pair_diag(o2, BAND_Q).T.astype(o_ref.dtype)

    scores(0)
    for i in range(n_tiles):
        if i + 1 < n_tiles:
            scores(i + 1)
        finish(i)


def _band_prompt(k, q_t, v_t, gen):
    b, t, d = k.shape
    n_groups = d // LANES
    assert t % BAND_Q == 0 and BAND_Q % TIME_BLOCK == 0 and A_WINDOW % TIME_BLOCK == 0
    slab = pl.BlockSpec((1, t // TIME_BLOCK, LANES, TIME_BLOCK), lambda j, i: (i, 0, j, 0))
    std = pl.BlockSpec((1, t, LANES), lambda j, i: (i, 0, j))
    return pl.pallas_call(
        functools.partial(_band_kernel, n_tiles=t // BAND_Q),
        grid=(n_groups, b),
        in_specs=[std, slab, slab,
                  pl.BlockSpec((1, HEADS_PER_GROUP, gen.shape[2]), lambda j, i: (j, 0, 0))],
        out_specs=std,
        out_shape=jax.ShapeDtypeStruct((b, t, d), BF16),
        scratch_shapes=[pltpu.VMEM((2, BAND_W, HEADS_PER_GROUP * BAND_Q), F32),
                        pltpu.VMEM((BAND_W, HEADS_PER_GROUP * BAND_Q), F32)],
        compiler_params=_params("parallel", "arbitrary"),
        name="band_prompt",
    )(k, q_t, v_t, gen)


def _fox_kernel(k_ref, qt_ref, vt_ref, c_ref, o_ref, s_ref, kaug_ref, *, n_tiles):
    head_mask = _pair_mask(FOX_T)
    blks = FOX_T // TIME_BLOCK
    wide = HEADS_PER_GROUP * FOX_T
    head_cols = [slice(h * FOX_T, (h + 1) * FOX_T) for h in range(HEADS_PER_GROUP)]
    head_rows = [slice(h * HEAD_DIM, (h + 1) * HEAD_DIM) for h in range(HEADS_PER_GROUP)]
    causal = (lax.broadcasted_iota(jnp.int32, (FOX_T, wide), 0)
              <= lax.broadcasted_iota(jnp.int32, (FOX_T, wide), 1) % FOX_T)

    ones_row0 = N_SPLIT * HEADS_PER_GROUP

    def split3(x):
        hi = x.astype(BF16).astype(F32)
        mid = (x - hi).astype(BF16).astype(F32)
        return hi, mid, (x - hi - mid).astype(BF16).astype(F32)

    def bias_rows(values, width):
        row = lax.broadcasted_iota(jnp.int32, (BIAS_ROWS, width), 0)
        top = jnp.zeros((BIAS_ROWS, width), F32)
        for r, v in enumerate(values):
            top = jnp.where(row == r, v, top)
        return jnp.concatenate([top, jnp.zeros((LANES - BIAS_ROWS, width), F32)], axis=0)

    for j in range(n_tiles):
        parts = split3(c_ref[0, 0, :, j * FOX_T:(j + 1) * FOX_T] * (-LOG2E))
        values = [parts[r // HEADS_PER_GROUP][r % HEADS_PER_GROUP:r % HEADS_PER_GROUP + 1] for r in range(ones_row0)]
        values += [jnp.ones((1, FOX_T), F32)] * N_SPLIT
        kaug_ref[j] = bias_rows(values, FOX_T).T.astype(BF16)

    col_head = lax.broadcasted_iota(jnp.int32, (1, wide), 1) // FOX_T
    q2_cache = {}

    def scores(n, i, j):
        if i not in q2_cache:
            q = _q_tile(qt_ref, i * blks, blks)
            c0 = split3(c_ref[0, 0, :, i * FOX_T:i * FOX_T + 1] * LOG2E)
            values = [(col_head == r % HEADS_PER_GROUP).astype(F32) for r in range(ones_row0)]
            values += [jnp.where(col_head == 0, part[0:1], part[1:2]) for part in c0]
            q2_cache.clear()
            q2_cache[i] = jnp.concatenate([jnp.concatenate([q] * HEADS_PER_GROUP, axis=1) * head_mask,
                                           bias_rows(values, wide).astype(BF16)], axis=0)
        keys = jnp.concatenate([k_ref[0, j * FOX_T:(j + 1) * FOX_T, :], kaug_ref[j]], axis=1)
        s = _dot(keys, q2_cache[i])
        if j == i:
            s = jnp.where(causal, s, NEG_INF)
        s_ref[n % 2] = s

    def update(n, j, carry):
        m, accs = carry
        s = s_ref[n % 2]
        m_new = jnp.maximum(m, jnp.max(s, axis=0, keepdims=True))
        alpha = jnp.exp2(m - m_new)
        p = jnp.exp2(s - m_new).astype(BF16)
        accs = tuple(alpha[:, head_cols[h]] * accs[h]
                     + _v_dot_p(vt_ref, j * blks, p[:, head_cols[h]], head_rows[h])
                     for h in range(HEADS_PER_GROUP))
        return m_new, accs

    pairs = [(i, j) for i in range(n_tiles) for j in range(i + 1)]
    scores(0, *pairs[0])
    carry = None
    for n, (i, j) in enumerate(pairs):
        if n + 1 < len(pairs):
            scores(n + 1, *pairs[n + 1])
        if j == 0:
            carry = (jnp.full((1, wide), NEG_INF, F32),
                     tuple(jnp.zeros((HEAD_DIM + ONES_ROWS, FOX_T), F32) for _ in range(HEADS_PER_GROUP)))
        carry = update(n, j, carry)
        if j == i:
            out = jnp.concatenate([acc[:HEAD_DIM] / acc[HEAD_DIM:HEAD_DIM + 1] for acc in carry[1]], axis=0)
            o_ref[0, i * FOX_T:(i + 1) * FOX_T, :] = out.T.astype(o_ref.dtype)


def _fox_prompt(k, q_t, v_t, c_rows):
    b, t, d = k.shape
    n_groups = d // LANES
    assert t % FOX_T == 0 and FOX_T % TIME_BLOCK == 0 and (N_SPLIT + 1) * HEADS_PER_GROUP <= BIAS_ROWS
    slab = pl.BlockSpec((1, t // TIME_BLOCK, LANES, TIME_BLOCK), lambda i, j: (i, 0, j, 0))
    std = pl.BlockSpec((1, t, LANES), lambda i, j: (i, 0, j))
    return pl.pallas_call(
        functools.partial(_fox_kernel, n_tiles=t // FOX_T),
        grid=(b, n_groups),
        in_specs=[std, slab, slab,
                  pl.BlockSpec((1, 1, HEADS_PER_GROUP, t), lambda i, j: (i, j, 0, 0))],
        out_specs=std,
        out_shape=jax.ShapeDtypeStruct((b, t, d), BF16),
        scratch_shapes=[pltpu.VMEM((2, FOX_T, HEADS_PER_GROUP * FOX_T), F32),
                        pltpu.VMEM((t // FOX_T, FOX_T, LANES), BF16)],
        compiler_params=_params("parallel", "parallel"),
        name="fox_prompt",
    )(k, q_t, v_t, c_rows)


def _stack_heads(q, n_heads):
    lane_head = lax.broadcasted_iota(jnp.int32, (1, q.shape[1]), 1) // HEAD_DIM
    return jnp.concatenate([q * (lane_head == h).astype(q.dtype) for h in range(n_heads)], axis=0)


def _unstack_heads(o, n_heads):
    t = o.shape[0] // n_heads
    lane_head = lax.broadcasted_iota(jnp.int32, (1, o.shape[1]), 1) // HEAD_DIM
    out = o[0:t]
    for h in range(1, n_heads):
        out = jnp.where(lane_head == h, o[h * t:(h + 1) * t], out)
    return out


def _softmax_step(carry, s, v_t):
    m, l, acc = carry
    m_new = jnp.maximum(m, jnp.max(s, axis=-1, keepdims=True))
    alpha = jnp.exp2(m - m_new)
    p = jnp.exp2(s - m_new)
    l = alpha * l + jnp.sum(p, axis=-1, keepdims=True)
    acc = alpha * acc + _dot_nt(p.astype(BF16), v_t)
    return m_new, l, acc


def _band_sample_kernel(q_ref, kc_ref, vc_ref, kn_ref, vn_ref, bias_ref, o_ref, *, n_heads):
    w = kc_ref.shape[2]
    qs = _stack_heads(q_ref[0], n_heads)
    s_c = _dot(qs, kc_ref[0].astype(BF16)) + bias_ref[:, 0:w]
    s_n = _dot(qs, kn_ref[0]) + bias_ref[:, w:w + SAMPLE_KEY_PAD]
    m = jnp.maximum(jnp.max(s_c, axis=-1, keepdims=True), jnp.max(s_n, axis=-1, keepdims=True))
    p_c = jnp.exp2(s_c - m)
    p_n = jnp.exp2(s_n - m)
    l = jnp.sum(p_c, axis=-1, keepdims=True) + jnp.sum(p_n, axis=-1, keepdims=True)
    o = (_dot_nt(p_c.astype(BF16), vc_ref[0].astype(BF16)) + _dot_nt(p_n.astype(BF16), vn_ref[0])) / l
    o_ref[0] = _unstack_heads(o, n_heads).astype(o_ref.dtype)


def _band_sample(q, k_cache_t, v_cache_t, k_new_t, v_new_t, bias):
    b, t, d = q.shape
    w = k_cache_t.shape[2]
    n_heads = d // HEAD_DIM
    feat = lambda cols: pl.BlockSpec((1, d, cols), lambda i: (i, 0, 0))
    return pl.pallas_call(
        functools.partial(_band_sample_kernel, n_heads=n_heads),
        grid=(b,),
        in_specs=[pl.BlockSpec((1, t, d), lambda i: (i, 0, 0)), feat(w), feat(w),
                  feat(SAMPLE_KEY_PAD), feat(SAMPLE_KEY_PAD), _const_spec(bias.shape)],
        out_specs=pl.BlockSpec((1, t, d), lambda i: (i, 0, 0)),
        out_shape=jax.ShapeDtypeStruct((b, t, d), BF16),
        compiler_params=_params("parallel"),
        name="band_sample",
    )(q, k_cache_t, v_cache_t, k_new_t, v_new_t, bias)


def _fox_sample_kernel(q_ref, kc_ref, vc_ref, kn_ref, vn_ref, ckc_ref, ckn_ref, cq_ref, o_ref,
                       qs_ref, m_ref, l_ref, acc_ref, *, n_heads, t):
    j = pl.program_id(1)
    rows = n_heads * t

    @pl.when(j == 0)
    def _():
        qs_ref[...] = _stack_heads(q_ref[0], n_heads)
        m_ref[...] = jnp.full(m_ref.shape, NEG_INF, F32)
        l_ref[...] = jnp.zeros(l_ref.shape, F32)
        acc_ref[...] = jnp.zeros(acc_ref.shape, F32)

    def key_cumsum(ck_ref):
        width = ck_ref.shape[2]
        return jnp.concatenate([jnp.broadcast_to(ck_ref[0, h:h + 1, :], (t, width)) for h in range(n_heads)], axis=0)

    def update(s, v_t):
        m, l, acc = _softmax_step((m_ref[...], l_ref[...], acc_ref[...]), s, v_t)
        m_ref[...] = m
        l_ref[...] = l
        acc_ref[...] = acc

    cq = cq_ref[0]
    s = _dot(qs_ref[...], kc_ref[0].astype(BF16)) + (cq - key_cumsum(ckc_ref)) * LOG2E
    update(s, vc_ref[0].astype(BF16))

    @pl.when(j == pl.num_programs(1) - 1)
    def _():
        s_n = _dot(qs_ref[...], kn_ref[0]) + (cq - key_cumsum(ckn_ref)) * LOG2E
        q_pos = lax.broadcasted_iota(jnp.int32, (rows, SAMPLE_KEY_PAD), 0) % t
        k_pos = lax.broadcasted_iota(jnp.int32, (rows, SAMPLE_KEY_PAD), 1)
        update(jnp.where(k_pos <= q_pos, s_n, NEG_INF), vn_ref[0])
        o_ref[0] = _unstack_heads(acc_ref[...] / l_ref[...], n_heads).astype(o_ref.dtype)


def _fox_sample(q, k_cache_t, v_cache_t, k_new_t, v_new_t, c_keys, c_q):
    b, t, d = q.shape
    past = k_cache_t.shape[2]
    n_heads = d // HEAD_DIM
    kb = min(FOX_S_KBLOCK, past)
    assert past % kb == 0 and kb % SAMPLE_KEY_PAD == 0 and t <= SAMPLE_KEY_PAD
    rows = n_heads * t
    per_b = lambda shape: pl.BlockSpec((1,) + shape, lambda i, j: (i, 0, 0))
    cache = pl.BlockSpec((1, d, kb), lambda i, j: (i, 0, j))
    return pl.pallas_call(
        functools.partial(_fox_sample_kernel, n_heads=n_heads, t=t),
        grid=(b, past // kb),
        in_specs=[per_b((t, d)), cache, cache, per_b((d, SAMPLE_KEY_PAD)), per_b((d, SAMPLE_KEY_PAD)),
                  pl.BlockSpec((1, n_heads, kb), lambda i, j: (i, 0, j)),
                  pl.BlockSpec((1, n_heads, SAMPLE_KEY_PAD), lambda i, j: (i, 0, past // SAMPLE_KEY_PAD)),
                  per_b((rows, 1))],
        out_specs=per_b((t, d)),
        out_shape=jax.ShapeDtypeStruct((b, t, d), BF16),
        scratch_shapes=[pltpu.VMEM((rows, d), BF16), pltpu.VMEM((rows, 1), F32),
                        pltpu.VMEM((rows, 1), F32), pltpu.VMEM((rows, d), F32)],
        compiler_params=_params("parallel", "arbitrary"),
        name="fox_sample",
    )(q, k_cache_t, v_cache_t, k_new_t, v_new_t, c_keys, c_keys, c_q)


def _tail_kernel(x_ref, oa_ref, ob_ref, p_ref, woa_ref, wob_ref, nf_ref, wg_ref, wu_ref, wd_ref,
                 np_ref, wpg_ref, wpp_ref, y_ref, *, n_ff_chunks):
    x = x_ref[...] + _dot(oa_ref[...], woa_ref[...]) + _dot(ob_ref[...], wob_ref[...])
    h = _rms(x, nf_ref[...]).astype(BF16)
    ffn = None
    for c in range(n_ff_chunks):
        cols = slice(c * FF_CHUNK, (c + 1) * FF_CHUNK)
        g = _dot(h, wg_ref[:, cols])
        u = _dot(h, wu_ref[:, cols])
        part = _dot((g * jax.nn.sigmoid(g) * u).astype(BF16), wd_ref[cols, :])
        ffn = part if ffn is None else ffn + part
    x = x + ffn
    gate = jax.nn.sigmoid(_dot(_rms(x, np_ref[...]).astype(BF16), wpg_ref[...]))
    y_ref[...] = x + _dot(p_ref[...].astype(BF16), wpp_ref[...]) * gate


def _tail(x, oa, ob, p, woa, wob, nf, wg, wu, wd, npl, wpg, wpp):
    n, d = x.shape
    d_ff = wg.shape[1]
    tm = min(ROW_TILE, n)
    assert n % tm == 0 and d_ff % FF_CHUNK == 0
    row = lambda a: pl.BlockSpec((tm, a.shape[1]), lambda i: (i, 0))
    weights = (woa, wob, nf, wg, wu, wd, npl, wpg, wpp)
    return pl.pallas_call(
        functools.partial(_tail_kernel, n_ff_chunks=d_ff // FF_CHUNK),
        grid=(n // tm,),
        in_specs=[row(x), row(oa), row(ob), row(p)] + [_const_spec(w.shape) for w in weights],
        out_specs=row(x),
        out_shape=jax.ShapeDtypeStruct((n, d), F32),
        compiler_params=_params("parallel"),
        name="tail",
    )(x, oa, ob, p, *weights)


def _band_bias_gen(table):
    table = table.astype(F32) * LOG2E
    n = BAND_W + BAND_Q - 1
    edge = BAND_W - 1 - A_WINDOW - REL_CLIP
    return jnp.concatenate([jnp.repeat(table[:, :1], edge, axis=1), table,
                            jnp.repeat(table[:, -1:], n - edge - table.shape[1], axis=1),
                            jnp.zeros((table.shape[0], 1), F32)], axis=1)


def _band_bias_sample(gen, st, n_keys):
    rev = gen[:, ::-1]
    rows = jnp.stack([rev[:, BAND_Q - t:BAND_Q - t + BAND_W] for t in range(st)], axis=1)
    return jnp.where(jnp.arange(BAND_W)[None, None, :] < n_keys, rows, NEG_INF)


def _feature_major(cache):
    b, t, n, hd = cache.shape
    return cache.transpose(0, 2, 3, 1).reshape(b, n * hd, t)


def _time_major(x_t, n_heads):
    b, _, t = x_t.shape
    return x_t.reshape(b, n_heads, HEAD_DIM, t).transpose(0, 3, 1, 2)


def kernel(x_prompt, x_sample, cache_k_a, cache_v_a, cache_k_b, cache_v_b, cache_logf_b, p_prompt, p_sample, norm_mix, w_in, b_f, q_norm_a, k_norm_a, q_norm_b, k_norm_b, rel_bias_a, w_out, norm_ffn, w_gate, w_up, w_down, norm_ple, w_ple_gate, w_ple_proj):
    depth = w_in.shape[0]
    b, t, d = x_prompt.shape
    sb, st, _ = x_sample.shape
    n_ha, n_hb = cache_k_a.shape[3], cache_k_b.shape[3]
    d_a, d_b = n_ha * HEAD_DIM, n_hb * HEAD_DIM
    past = cache_k_b.shape[2]
    w_a = cache_k_a.shape[2]
    n_s = sb * st
    assert d_a == d_b and w_a == A_WINDOW and st <= CHUNK and t >= A_WINDOW and n_s % ROW_TILE == 0
    n_groups = n_hb // HEADS_PER_GROUP

    xp = x_prompt
    xs = x_sample.reshape(1, n_s, d)
    outs = [[] for _ in range(10)]
    for i in range(depth):
        wt = w_in[i].T.astype(BF16)
        bf_t = jnp.broadcast_to(b_f[i][:, None], (n_hb, ROW_TILE)).astype(F32)
        gains_t = jnp.stack([jnp.broadcast_to(g[i][:, None], (HEAD_DIM, ROW_TILE))
                             for g in (q_norm_a, k_norm_a, q_norm_b, k_norm_b)]).astype(F32)
        g_mix = norm_mix[i].reshape(1, d)
        tail_w = (w_out[i, :d_a].astype(BF16), w_out[i, d_a:].astype(BF16), norm_ffn[i].reshape(1, d),
                  w_gate[i].astype(BF16), w_up[i].astype(BF16), w_down[i].astype(BF16),
                  norm_ple[i].reshape(1, d), w_ple_gate[i].astype(BF16), w_ple_proj[i].astype(BF16))
        bias_gen = _band_bias_gen(rel_bias_a[i])

        qa_t, qb_t, ka16, kb16, va_t16, vb_t16, ka_t, va_t, kb_t, vb_t, lf_t = _proj(
            xp, g_mix, wt, bf_t, gains_t, n_hb, A_WINDOW)
        c = _cumsum_lanes(lf_t.reshape(b * n_hb, t)).reshape(b, n_groups, HEADS_PER_GROUP, t)
        o_a = _band_prompt(ka16, qa_t, va_t16, bias_gen.reshape(n_ha // HEADS_PER_GROUP, HEADS_PER_GROUP, -1))
        o_b = _fox_prompt(kb16, qb_t, vb_t16, c)
        xp = _tail(xp.reshape(b * t, d), o_a.reshape(b * t, d_a), o_b.reshape(b * t, d_b),
                   p_prompt[i].reshape(b * t, -1), *tail_w).reshape(b, t, d)
        outs[0].append(_time_major(ka_t, n_ha))
        outs[1].append(_time_major(va_t, n_ha))
        outs[2].append(_time_major(kb_t, n_hb))
        outs[3].append(_time_major(vb_t, n_hb))
        outs[4].append(lf_t.transpose(0, 2, 1))

        qa_t, qb_t, ka16, kb16, va_t16, vb_t16, ka_t, va_t, kb_t, vb_t, lf_t = _proj(
            xs, g_mix, wt, bf_t, gains_t, n_hb, n_s)
        per_stream = lambda x_t: x_t.reshape(-1, sb, st).transpose(1, 0, 2)
        unslab = lambda s_t: s_t[0].transpose(1, 0, 2).reshape(s_t.shape[2], n_s)
        new_keys = lambda x_t: jnp.pad(per_stream(x_t).astype(BF16), ((0, 0), (0, 0), (0, SAMPLE_KEY_PAD - st)))
        q_rows = lambda s_t: unslab(s_t).T.reshape(sb, st, -1)
        lf_s = per_stream(lf_t[0])
        lf_all = jnp.concatenate([cache_logf_b[i].astype(F32).transpose(0, 2, 1), lf_s], axis=2)
        t_all = past + SAMPLE_KEY_PAD
        lf_all = jnp.pad(lf_all, ((0, 0), (0, 0), (0, t_all - past - st))).reshape(sb * n_hb, t_all)
        c_all = _cumsum_lanes(lf_all).reshape(sb, n_hb, t_all)
        c_q = c_all[:, :, past:past + st].reshape(sb, n_hb * st, 1)
        bias_s = _band_bias_sample(bias_gen, st, w_a + st)
        o_a = _band_sample(q_rows(qa_t), _feature_major(cache_k_a[i]), _feature_major(cache_v_a[i]),
                           new_keys(ka_t[0]), new_keys(va_t[0]), bias_s.reshape(n_ha * st, BAND_W))
        o_b = _fox_sample(q_rows(qb_t), _feature_major(cache_k_b[i]), _feature_major(cache_v_b[i]),
                          new_keys(kb_t[0]), new_keys(vb_t[0]), c_all, c_q)
        xs = _tail(xs.reshape(n_s, d), o_a.reshape(n_s, d_a), o_b.reshape(n_s, d_b),
                   p_sample[i].reshape(n_s, -1), *tail_w).reshape(1, n_s, d)
        outs[5].append(_time_major(per_stream(ka_t[0]), n_ha))
        outs[6].append(_time_major(per_stream(va_t[0]), n_ha))
        outs[7].append(_time_major(per_stream(kb_t[0]), n_hb))
        outs[8].append(_time_major(per_stream(vb_t[0]), n_hb))
        outs[9].append(lf_s.transpose(0, 2, 1))

    return (xp, xs.reshape(sb, st, d)) + tuple(jnp.stack(o) for o in outs)
```

```python
import functools

import jax
import jax.numpy as jnp
from jax import lax
from jax.experimental import pallas as pl
from jax.experimental.pallas import tpu as pltpu

F32 = jnp.float32
BF16 = jnp.bfloat16

CHUNK = 64
HEAD_DIM = 64
A_LEFT_CHUNKS = 8
A_WINDOW = A_LEFT_CHUNKS * CHUNK
REL_CLIP = 128
ATTN_SCALE = HEAD_DIM ** -0.5
LOG2E = 1.4426950408889634
Q_SCALE = ATTN_SCALE * LOG2E
NEG_INF = -1e30
EPS = 1e-6

LANES = 128
HEADS_PER_GROUP = LANES // HEAD_DIM
VMEM_LIMIT = 56 * 1024 * 1024

ROW_TILE = 512
TIME_BLOCK = 128
BAND_Q = 2 * CHUNK
BAND_PAIR = 2
BAND_W = A_WINDOW + BAND_Q
FOX_T = 512
FOX_H = FOX_T // 2
FF_CHUNK = 256
ONES_ROWS = 16
BIAS_ROWS = 16
N_SPLIT = 3
SAMPLE_KEY_PAD = 128
FOX_S_KBLOCK = 2048
FOX_S_SUB = 512


def _params(*sem):
    return pltpu.CompilerParams(dimension_semantics=sem, vmem_limit_bytes=VMEM_LIMIT)


def _const_spec(shape):
    zeros = (0,) * len(shape)
    return pl.BlockSpec(shape, lambda *_: zeros, pipeline_mode=pl.Buffered(1))


def _rms(x, gain):
    ms = jnp.mean(x * x, axis=-1, keepdims=True)
    return x * lax.rsqrt(ms + EPS) * gain


def _dot(a, b):
    return jnp.dot(a, b, preferred_element_type=F32)


def _dot_nt(a, b):
    return lax.dot_general(a, b, (((1,), (1,)), ((), ())), preferred_element_type=F32)


def _proj_kernel(x_ref, g_ref, wt_ref, bf_ref, gains_ref,
                 qa_ref, qb_ref, ka16_ref, kb16_ref, va16_ref, vb16_ref,
                 ka_ref, va_ref, kb_ref, vb_ref, lf_ref, *, d_seg, n_fh, keep_from):
    h = _rms(x_ref[0], g_ref[...]).astype(BF16)
    n_heads = d_seg // HEAD_DIM
    n_slabs = qa_ref.shape[1]

    def seg(s, extra=0):
        return _dot_nt(wt_ref[s * d_seg:(s + 1) * d_seg + extra, :], h)

    def head_norm(y, gi):
        rows = []
        for hd in range(n_heads):
            yh = y[hd * HEAD_DIM:(hd + 1) * HEAD_DIM, :]
            ms = jnp.sum(yh * yh, axis=0, keepdims=True) * (1.0 / HEAD_DIM)
            rows.append(yh * lax.rsqrt(ms + EPS) * gains_ref[gi])
        return jnp.concatenate(rows, axis=0)

    def put_slabs(ref, y):
        y = y.astype(BF16)
        for u in range(n_slabs):
            ref[0, u] = y[:, u * TIME_BLOCK:(u + 1) * TIME_BLOCK]

    put_slabs(qa_ref, head_norm(seg(0), 0) * Q_SCALE)
    ka = head_norm(seg(1), 1)
    ka16_ref[0] = ka.T.astype(BF16)
    va = seg(2)
    put_slabs(va16_ref, va)

    @pl.when(pl.program_id(1) >= keep_from)
    def _():
        ka_ref[0] = ka
        va_ref[0] = va

    put_slabs(qb_ref, head_norm(seg(3), 2) * Q_SCALE)
    kb = head_norm(seg(4), 3)
    kb_ref[0] = kb
    kb16_ref[0] = kb.T.astype(BF16)
    tail = seg(5, n_fh)
    vb = tail[:d_seg]
    vb_ref[0] = vb
    put_slabs(vb16_ref, vb)
    z = tail[d_seg:] + bf_ref[...]
    lf_ref[0] = jnp.minimum(z, 0.0) - jnp.log1p(jnp.exp(-jnp.abs(z)))


def _proj(x, g, wt, bf_t, gains_t, n_fh, keep):
    b, t, d = x.shape
    d_seg = (wt.shape[0] - n_fh) // 6
    tm = ROW_TILE
    assert t % tm == 0 and keep % tm == 0 and tm % TIME_BLOCK == 0
    nt, n_slabs = t // tm, tm // TIME_BLOCK
    keep_from = nt - keep // tm
    feat = lambda rows: pl.BlockSpec((1, rows, tm), lambda i, j: (i, 0, j))
    slab = pl.BlockSpec((1, n_slabs, d_seg, TIME_BLOCK), lambda i, j: (i, j, 0, 0))
    kept = pl.BlockSpec((1, d_seg, tm), lambda i, j: (i, 0, jnp.maximum(j - keep_from, 0)))
    std = pl.BlockSpec((1, tm, d_seg), lambda i, j: (i, j, 0))
    slab_out = jax.ShapeDtypeStruct((b, t // TIME_BLOCK, d_seg, TIME_BLOCK), BF16)
    std_out = jax.ShapeDtypeStruct((b, t, d_seg), BF16)
    feat_out = jax.ShapeDtypeStruct((b, d_seg, t), F32)
    kept_out = jax.ShapeDtypeStruct((b, d_seg, keep), F32)
    return pl.pallas_call(
        functools.partial(_proj_kernel, d_seg=d_seg, n_fh=n_fh, keep_from=keep_from),
        grid=(b, nt),
        in_specs=[pl.BlockSpec((1, tm, d), lambda i, j: (i, j, 0)), _const_spec(g.shape), _const_spec(wt.shape),
                  _const_spec(bf_t.shape), _const_spec(gains_t.shape)],
        out_specs=[slab, slab, std, std, slab, slab, kept, kept, feat(d_seg), feat(d_seg), feat(n_fh)],
        out_shape=[slab_out, slab_out, std_out, std_out, slab_out, slab_out,
                   kept_out, kept_out, feat_out, feat_out, jax.ShapeDtypeStruct((b, n_fh, t), F32)],
        compiler_params=_params("parallel", "arbitrary"),
        name="proj",
    )(x, g, wt, bf_t, gains_t)


def _cumsum_kernel(x_ref, u_ref, c_ref, *, n_blocks):
    u = u_ref[...]
    off = jnp.zeros((x_ref.shape[0], 1), F32)
    for j in range(n_blocks):
        x = x_ref[:, j * LANES:(j + 1) * LANES]
        hi = x.astype(BF16)
        r1 = x - hi.astype(F32)
        mid = r1.astype(BF16)
        lo = (r1 - mid.astype(F32)).astype(BF16)
        c = (_dot(hi, u) + _dot(mid, u) + _dot(lo, u)) + off
        c_ref[:, j * LANES:(j + 1) * LANES] = c
        off = c[:, LANES - 1:LANES]


def _cumsum_lanes(x):
    g, t = x.shape
    assert t % LANES == 0
    idx = jnp.arange(LANES)
    u = (idx[:, None] <= idx[None, :]).astype(BF16)
    return pl.pallas_call(
        functools.partial(_cumsum_kernel, n_blocks=t // LANES),
        out_shape=jax.ShapeDtypeStruct((g, t), F32),
        compiler_params=pltpu.CompilerParams(vmem_limit_bytes=VMEM_LIMIT),
        name="cumsum",
    )(x, u)


def _pair_mask(tq):
    shape = (LANES, HEADS_PER_GROUP * tq)
    row_head = lax.broadcasted_iota(jnp.int32, shape, 0) // HEAD_DIM
    col_head = lax.broadcasted_iota(jnp.int32, shape, 1) // tq
    return (row_head == col_head).astype(F32).astype(BF16)


def _pair_diag(o2, tq):
    return jnp.concatenate([o2[h * HEAD_DIM:(h + 1) * HEAD_DIM, h * tq:(h + 1) * tq]
                            for h in range(HEADS_PER_GROUP)], axis=0)


def _v_dot_p(v_ref, blk0, p, rows=slice(None)):
    keys = p.shape[0]
    v = jnp.concatenate([v_ref[0, blk0 + u, rows, :] for u in range(keys // TIME_BLOCK)], axis=1)
    return _dot(jnp.concatenate([v, jnp.ones((ONES_ROWS, keys), BF16)], axis=0), p)


def _q_tile(q_ref, blk0, n_blk):
    parts = [q_ref[0, blk0 + u] for u in range(n_blk)]
    return parts[0] if n_blk == 1 else jnp.concatenate(parts, axis=1)


def _band_kernel(k_ref, qt_ref, vt_ref, gen_ref, o_ref, s_ref, bias_ref, *, n_tiles):
    head_mask = _pair_mask(BAND_Q)
    q_blks = BAND_Q // TIME_BLOCK
    period = gen_ref.shape[2]

    @pl.when(pl.program_id(1) == 0)
    def _():
        j = lax.broadcasted_iota(jnp.int32, (BAND_W, BAND_Q), 0)
        r = lax.broadcasted_iota(jnp.int32, (BAND_W, BAND_Q), 1)
        gap = j // CHUNK - r // CHUNK
        in_band = (gap >= 0) & (gap <= A_LEFT_CHUNKS)
        for h in range(HEADS_PER_GROUP):
            rows = jnp.broadcast_to(gen_ref[0, h:h + 1, :], (BAND_W, period))
            toeplitz = pltpu.roll(rows, period - BAND_W + 1, 1, stride=1, stride_axis=0)[:, :BAND_Q]
            bias_ref[:, h * BAND_Q:(h + 1) * BAND_Q] = jnp.where(in_band, toeplitz, NEG_INF)

    def window(i):
        k0 = max(i * BAND_Q - A_WINDOW, 0)
        return k0, (i + 1) * BAND_Q - k0

    def scores(n):
        maxes = []
        for a in range(BAND_PAIR):
            i = n * BAND_PAIR + a
            k0, width = window(i)
            q = _q_tile(qt_ref, i * q_blks, q_blks)
            q2 = jnp.concatenate([q] * HEADS_PER_GROUP, axis=1) * head_mask
            s = _dot(k_ref[0, k0:k0 + width, :], q2) + bias_ref[BAND_W - width:, :]
            s_ref[n % 2, a, 0:width, :] = s
            maxes.append(jnp.max(s, axis=0, keepdims=True))
        return maxes

    def finish(n, maxes):
        outs = []
        for a in range(BAND_PAIR):
            i = n * BAND_PAIR + a
            k0, width = window(i)
            p = jnp.exp2(s_ref[n % 2, a, 0:width, :] - maxes[a])
            o2 = _v_dot_p(vt_ref, k0 // TIME_BLOCK, p.astype(BF16))
            outs.append(_pair_diag(o2[:LANES] / o2[LANES:LANES + 1], BAND_Q))
        rows = BAND_PAIR * BAND_Q
        o_ref[0, n * rows:(n + 1) * rows, :] = jnp.concatenate(outs, axis=1).T.astype(o_ref.dtype)

    n_steps = n_tiles // BAND_PAIR
    maxes = scores(0)
    for n in range(n_steps):
        nxt = scores(n + 1) if n + 1 < n_steps else None
        finish(n, maxes)
        maxes = nxt


def _band_prompt(k, q_t, v_t, gen):
    b, t, d = k.shape
    n_groups = d // LANES
    assert t % (BAND_PAIR * BAND_Q) == 0 and BAND_Q % TIME_BLOCK == 0 and A_WINDOW % TIME_BLOCK == 0
    slab = pl.BlockSpec((1, t // TIME_BLOCK, LANES, TIME_BLOCK), lambda j, i: (i, 0, j, 0))
    std = pl.BlockSpec((1, t, LANES), lambda j, i: (i, 0, j))
    return pl.pallas_call(
        functools.partial(_band_kernel, n_tiles=t // BAND_Q),
        grid=(n_groups, b),
        in_specs=[std, slab, slab,
                  pl.BlockSpec((1, HEADS_PER_GROUP, gen.shape[2]), lambda j, i: (j, 0, 0))],
        out_specs=std,
        out_shape=jax.ShapeDtypeStruct((b, t, d), BF16),
        scratch_shapes=[pltpu.VMEM((2, BAND_PAIR, BAND_W, HEADS_PER_GROUP * BAND_Q), F32),
                        pltpu.VMEM((BAND_W, HEADS_PER_GROUP * BAND_Q), F32)],
        compiler_params=_params("parallel", "arbitrary"),
        name="band_prompt",
    )(k, q_t, v_t, gen)


def _fox_kernel(k_ref, qt_ref, vt_ref, c_ref, o_ref, s_ref, kaug_ref, *, n_tiles):
    n_halves = FOX_T // FOX_H
    n_cb = n_halves * HEADS_PER_GROUP
    blks = FOX_T // TIME_BLOCK
    wide = n_cb * FOX_H
    col_blocks = [slice(c * FOX_H, (c + 1) * FOX_H) for c in range(n_cb)]
    head_rows = [slice(h * HEAD_DIM, (h + 1) * HEAD_DIM) for h in range(HEADS_PER_GROUP)]
    row_head = lax.broadcasted_iota(jnp.int32, (LANES, wide), 0) // HEAD_DIM
    col_head = (lax.broadcasted_iota(jnp.int32, (1, wide), 1) // FOX_H) % HEADS_PER_GROUP
    head_mask = (row_head == col_head).astype(F32).astype(BF16)
    key_pos = lax.broadcasted_iota(jnp.int32, (FOX_H, wide), 0)
    col = lax.broadcasted_iota(jnp.int32, (FOX_H, wide), 1)
    causal_lo = (key_pos <= col % FOX_H) | (col >= HEADS_PER_GROUP * FOX_H)
    causal_hi = causal_lo[:, :HEADS_PER_GROUP * FOX_H]

    ones_row0 = N_SPLIT * HEADS_PER_GROUP

    def split3(x):
        hi = x.astype(BF16).astype(F32)
        mid = (x - hi).astype(BF16).astype(F32)
        return hi, mid, (x - hi - mid).astype(BF16).astype(F32)

    def bias_rows(values, width):
        row = lax.broadcasted_iota(jnp.int32, (BIAS_ROWS, width), 0)
        top = jnp.zeros((BIAS_ROWS, width), F32)
        for r, v in enumerate(values):
            top = jnp.where(row == r, v, top)
        return jnp.concatenate([top, jnp.zeros((LANES - BIAS_ROWS, width), F32)], axis=0)

    for j in range(n_tiles):
        parts = split3(c_ref[0, 0, :, j * FOX_T:(j + 1) * FOX_T] * (-LOG2E))
        values = [parts[r // HEADS_PER_GROUP][r % HEADS_PER_GROUP:r % HEADS_PER_GROUP + 1] for r in range(ones_row0)]
        values += [jnp.ones((1, FOX_T), F32)] * N_SPLIT
        kaug_ref[j] = bias_rows(values, FOX_T).T.astype(BF16)

    q2_cache = {}

    def keys_of(c, diag):
        return FOX_H if diag and c < HEADS_PER_GROUP else FOX_T

    def scores(n, i, j):
        if i not in q2_cache:
            q = _q_tile(qt_ref, i * blks, blks)
            q = jnp.concatenate([q[:, (c // HEADS_PER_GROUP) * FOX_H:(c // HEADS_PER_GROUP + 1) * FOX_H]
                                 for c in range(n_cb)], axis=1)
            c0 = split3(c_ref[0, 0, :, i * FOX_T:i * FOX_T + 1] * LOG2E)
            values = [(col_head == r % HEADS_PER_GROUP).astype(F32) for r in range(ones_row0)]
            values += [jnp.where(col_head == 0, part[0:1], part[1:2]) for part in c0]
            q2_cache.clear()
            q2_cache[i] = jnp.concatenate([q * head_mask, bias_rows(values, wide).astype(BF16)], axis=0)
        q2 = q2_cache[i]
        keys = jnp.concatenate([k_ref[0, j * FOX_T:(j + 1) * FOX_T, :], kaug_ref[j]], axis=1)
        if j < i:
            s = _dot(keys, q2)
            s_ref[n % 2] = s
            return [jnp.max(s[:, cb], axis=0, keepdims=True) for cb in col_blocks]
        upper = HEADS_PER_GROUP * FOX_H
        s_lo = jnp.where(causal_lo, _dot(keys[:FOX_H], q2), NEG_INF)
        s_hi = jnp.where(causal_hi, _dot(keys[FOX_H:], q2[:, upper:]), NEG_INF)
        s_ref[n % 2, 0:FOX_H, :] = s_lo
        s_ref[n % 2, FOX_H:, upper:] = s_hi
        maxes = [jnp.max(s_lo[:, cb], axis=0, keepdims=True) for cb in col_blocks]
        for c in range(HEADS_PER_GROUP, n_cb):
            hi_max = jnp.max(s_hi[:, col_blocks[c - HEADS_PER_GROUP]], axis=0, keepdims=True)
            maxes[c] = jnp.maximum(maxes[c], hi_max)
        return maxes

    def update(n, i, j, maxes, carry):
        new = []
        for c, (m, acc) in enumerate(carry):
            nk = keys_of(c, j == i)
            m_new = jnp.maximum(m, maxes[c])
            p = jnp.exp2(s_ref[n % 2, 0:nk, col_blocks[c]] - m_new).astype(BF16)
            pv = _v_dot_p(vt_ref, j * blks, p, head_rows[c % HEADS_PER_GROUP])
            new.append((m_new, jnp.exp2(m - m_new) * acc + pv))
        return new

    pairs = [(i, j) for i in range(n_tiles) for j in range(i + 1)]
    maxes = scores(0, *pairs[0])
    carry = None
    for n, (i, j) in enumerate(pairs):
        nxt = scores(n + 1, *pairs[n + 1]) if n + 1 < len(pairs) else None
        if j == 0:
            carry = [(jnp.full((1, FOX_H), NEG_INF, F32), jnp.zeros((HEAD_DIM + ONES_ROWS, FOX_H), F32))
                     for _ in range(n_cb)]
        carry = update(n, i, j, maxes, carry)
        maxes = nxt
        if j == i:
            accs = [jnp.concatenate([carry[HEADS_PER_GROUP * half + h][1] for half in range(n_halves)], axis=1)
                    for h in range(HEADS_PER_GROUP)]
            out = jnp.concatenate([acc[:HEAD_DIM] / acc[HEAD_DIM:HEAD_DIM + 1] for acc in accs], axis=0)
            o_ref[0, i * FOX_T:(i + 1) * FOX_T, :] = out.T.astype(o_ref.dtype)


def _fox_prompt(k, q_t, v_t, c_rows):
    b, t, d = k.shape
    n_groups = d // LANES
    assert t % FOX_T == 0 and FOX_T % TIME_BLOCK == 0 and (N_SPLIT + 1) * HEADS_PER_GROUP <= BIAS_ROWS
    slab = pl.BlockSpec((1, t // TIME_BLOCK, LANES, TIME_BLOCK), lambda i, j: (i, 0, j, 0))
    std = pl.BlockSpec((1, t, LANES), lambda i, j: (i, 0, j))
    return pl.pallas_call(
        functools.partial(_fox_kernel, n_tiles=t // FOX_T),
        grid=(b, n_groups),
        in_specs=[std, slab, slab,
                  pl.BlockSpec((1, 1, HEADS_PER_GROUP, t), lambda i, j: (i, j, 0, 0))],
        out_specs=std,
        out_shape=jax.ShapeDtypeStruct((b, t, d), BF16),
        scratch_shapes=[pltpu.VMEM((2, FOX_T, HEADS_PER_GROUP * FOX_T), F32),
                        pltpu.VMEM((t // FOX_T, FOX_T, LANES), BF16)],
        compiler_params=_params("parallel", "parallel"),
        name="fox_prompt",
    )(k, q_t, v_t, c_rows)


def _stack_heads(q, n_heads):
    lane_head = lax.broadcasted_iota(jnp.int32, (1, q.shape[1]), 1) // HEAD_DIM
    return jnp.concatenate([q * (lane_head == h).astype(q.dtype) for h in range(n_heads)], axis=0)


def _unstack_heads(o, n_heads):
    t = o.shape[0] // n_heads
    lane_head = lax.broadcasted_iota(jnp.int32, (1, o.shape[1]), 1) // HEAD_DIM
    out = o[0:t]
    for h in range(1, n_heads):
        out = jnp.where(lane_head == h, o[h * t:(h + 1) * t], out)
    return out


def _softmax_step(carry, s, v_t):
    m, l, acc = carry
    m_new = jnp.maximum(m, jnp.max(s, axis=-1, keepdims=True))
    alpha = jnp.exp2(m - m_new)
    p = jnp.exp2(s - m_new)
    l = alpha * l + jnp.sum(p, axis=-1, keepdims=True)
    acc = alpha * acc + _dot_nt(p.astype(BF16), v_t)
    return m_new, l, acc


def _band_sample_kernel(q_ref, kc_ref, vc_ref, kn_ref, vn_ref, bias_ref, o_ref, *, n_heads):
    w = kc_ref.shape[2]
    qs = _stack_heads(q_ref[0], n_heads)
    s_c = _dot(qs, kc_ref[0].astype(BF16)) + bias_ref[:, 0:w]
    s_n = _dot(qs, kn_ref[0]) + bias_ref[:, w:w + SAMPLE_KEY_PAD]
    m = jnp.maximum(jnp.max(s_c, axis=-1, keepdims=True), jnp.max(s_n, axis=-1, keepdims=True))
    p_c = jnp.exp2(s_c - m)
    p_n = jnp.exp2(s_n - m)
    l = jnp.sum(p_c, axis=-1, keepdims=True) + jnp.sum(p_n, axis=-1, keepdims=True)
    o = (_dot_nt(p_c.astype(BF16), vc_ref[0].astype(BF16)) + _dot_nt(p_n.astype(BF16), vn_ref[0])) / l
    o_ref[0] = _unstack_heads(o, n_heads).astype(o_ref.dtype)


def _band_sample(q, k_cache_t, v_cache_t, k_new_t, v_new_t, bias):
    b, t, d = q.shape
    w = k_cache_t.shape[2]
    n_heads = d // HEAD_DIM
    feat = lambda cols: pl.BlockSpec((1, d, cols), lambda i: (i, 0, 0))
    return pl.pallas_call(
        functools.partial(_band_sample_kernel, n_heads=n_heads),
        grid=(b,),
        in_specs=[pl.BlockSpec((1, t, d), lambda i: (i, 0, 0)), feat(w), feat(w),
                  feat(SAMPLE_KEY_PAD), feat(SAMPLE_KEY_PAD), _const_spec(bias.shape)],
        out_specs=pl.BlockSpec((1, t, d), lambda i: (i, 0, 0)),
        out_shape=jax.ShapeDtypeStruct((b, t, d), BF16),
        compiler_params=_params("parallel"),
        name="band_sample",
    )(q, k_cache_t, v_cache_t, k_new_t, v_new_t, bias)


def _fox_sample_kernel(q_ref, kc_ref, vc_ref, kn_ref, vn_ref, ckc_ref, ckn_ref, cq_ref, o_ref,
                       qs_ref, m_ref, l_ref, acc_ref, s_ref, *, n_heads, t):
    j = pl.program_id(1)
    rows = n_heads * t
    sub = s_ref.shape[2]
    n_sub = kc_ref.shape[2] // sub

    @pl.when(j == 0)
    def _():
        qs_ref[...] = _stack_heads(q_ref[0], n_heads)
        m_ref[...] = jnp.full(m_ref.shape, NEG_INF, F32)
        l_ref[...] = jnp.zeros(l_ref.shape, F32)
        acc_ref[...] = jnp.zeros(acc_ref.shape, F32)

    def key_cumsum(ck_ref, cols):
        width = cols.stop - cols.start
        return jnp.concatenate([jnp.broadcast_to(ck_ref[0, h:h + 1, cols], (t, width)) for h in range(n_heads)], axis=0)

    def update(s, v_t):
        m, l, acc = _softmax_step((m_ref[...], l_ref[...], acc_ref[...]), s, v_t)
        m_ref[...] = m
        l_ref[...] = l
        acc_ref[...] = acc

    cq = cq_ref[0]

    def scores(u):
        cols = slice(u * sub, (u + 1) * sub)
        s_ref[u % 2] = (_dot(qs_ref[...], kc_ref[0, :, cols].astype(BF16))
                        + (cq - key_cumsum(ckc_ref, cols)) * LOG2E)

    scores(0)
    for u in range(n_sub):
        if u + 1 < n_sub:
            scores(u + 1)
        update(s_ref[u % 2], vc_ref[0, :, u * sub:(u + 1) * sub].astype(BF16))

    @pl.when(j == pl.num_programs(1) - 1)
    def _():
        s_n = _dot(qs_ref[...], kn_ref[0]) + (cq - key_cumsum(ckn_ref, slice(0, SAMPLE_KEY_PAD))) * LOG2E
        q_pos = lax.broadcasted_iota(jnp.int32, (rows, SAMPLE_KEY_PAD), 0) % t
        k_pos = lax.broadcasted_iota(jnp.int32, (rows, SAMPLE_KEY_PAD), 1)
        update(jnp.where(k_pos <= q_pos, s_n, NEG_INF), vn_ref[0])
        o_ref[0] = _unstack_heads(acc_ref[...] / l_ref[...], n_heads).astype(o_ref.dtype)


def _fox_sample(q, k_cache_t, v_cache_t, k_new_t, v_new_t, c_keys, c_q):
    b, t, d = q.shape
    past = k_cache_t.shape[2]
    n_heads = d // HEAD_DIM
    kb = min(FOX_S_KBLOCK, past)
    sub = min(FOX_S_SUB, kb)
    assert past % kb == 0 and kb % sub == 0 and sub % SAMPLE_KEY_PAD == 0 and t <= SAMPLE_KEY_PAD
    rows = n_heads * t
    per_b = lambda shape: pl.BlockSpec((1,) + shape, lambda i, j: (i, 0, 0))
    cache = pl.BlockSpec((1, d, kb), lambda i, j: (i, 0, j))
    return pl.pallas_call(
        functools.partial(_fox_sample_kernel, n_heads=n_heads, t=t),
        grid=(b, past // kb),
        in_specs=[per_b((t, d)), cache, cache, per_b((d, SAMPLE_KEY_PAD)), per_b((d, SAMPLE_KEY_PAD)),
                  pl.BlockSpec((1, n_heads, kb), lambda i, j: (i, 0, j)),
                  pl.BlockSpec((1, n_heads, SAMPLE_KEY_PAD), lambda i, j: (i, 0, past // SAMPLE_KEY_PAD)),
                  per_b((rows, 1))],
        out_specs=per_b((t, d)),
        out_shape=jax.ShapeDtypeStruct((b, t, d), BF16),
        scratch_shapes=[pltpu.VMEM((rows, d), BF16), pltpu.VMEM((rows, 1), F32),
                        pltpu.VMEM((rows, 1), F32), pltpu.VMEM((rows, d), F32),
                        pltpu.VMEM((2, rows, sub), F32)],
        compiler_params=_params("parallel", "arbitrary"),
        name="fox_sample",
    )(q, k_cache_t, v_cache_t, k_new_t, v_new_t, c_keys, c_keys, c_q)


def _tail_kernel(x_ref, oa_ref, ob_ref, p_ref, woa_ref, wob_ref, nf_ref, wg_ref, wu_ref, wd_ref,
                 np_ref, wpg_ref, wpp_ref, y_ref, *, n_ff_chunks):
    x = x_ref[...] + _dot(oa_ref[...], woa_ref[...]) + _dot(ob_ref[...], wob_ref[...])
    h = _rms(x, nf_ref[...]).astype(BF16)
    ffn = None
    for c in range(n_ff_chunks):
        cols = slice(c * FF_CHUNK, (c + 1) * FF_CHUNK)
        g = _dot(h, wg_ref[:, cols])
        u = _dot(h, wu_ref[:, cols])
        part = _dot((g * jax.nn.sigmoid(g) * u).astype(BF16), wd_ref[cols, :])
        ffn = part if ffn is None else ffn + part
    x = x + ffn
    gate = jax.nn.sigmoid(_dot(_rms(x, np_ref[...]).astype(BF16), wpg_ref[...]))
    y_ref[...] = x + _dot(p_ref[...].astype(BF16), wpp_ref[...]) * gate


def _tail(x, oa, ob, p, woa, wob, nf, wg, wu, wd, npl, wpg, wpp):
    n, d = x.shape
    d_ff = wg.shape[1]
    tm = min(ROW_TILE, n)
    assert n % tm == 0 and d_ff % FF_CHUNK == 0
    row = lambda a: pl.BlockSpec((tm, a.shape[1]), lambda i: (i, 0))
    weights = (woa, wob, nf, wg, wu, wd, npl, wpg, wpp)
    return pl.pallas_call(
        functools.partial(_tail_kernel, n_ff_chunks=d_ff // FF_CHUNK),
        grid=(n // tm,),
        in_specs=[row(x), row(oa), row(ob), row(p)] + [_const_spec(w.shape) for w in weights],
        out_specs=row(x),
        out_shape=jax.ShapeDtypeStruct((n, d), F32),
        compiler_params=_params("parallel"),
        name="tail",
    )(x, oa, ob, p, *weights)


def _band_bias_gen(table):
    table = table.astype(F32) * LOG2E
    n = BAND_W + BAND_Q - 1
    clipped = jnp.concatenate([jnp.repeat(table[:, :1], n, axis=1), table, jnp.repeat(table[:, -1:], n, axis=1)],
                              axis=1)
    first = n + REL_CLIP + A_WINDOW - (BAND_W - 1)
    return jnp.concatenate([clipped[:, first:first + n], jnp.zeros((table.shape[0], 1), F32)], axis=1)


def _band_bias_sample(gen, st, n_keys):
    rev = gen[:, ::-1]
    rows = jnp.stack([rev[:, BAND_Q - t:BAND_Q - t + BAND_W] for t in range(st)], axis=1)
    return jnp.where(jnp.arange(BAND_W)[None, None, :] < n_keys, rows, NEG_INF)


def _feature_major(cache):
    b, t, n, hd = cache.shape
    return cache.transpose(0, 2, 3, 1).reshape(b, n * hd, t)


def _time_major(x_t, n_heads):
    b, _, t = x_t.shape
    return x_t.reshape(b, n_heads, HEAD_DIM, t).transpose(0, 3, 1, 2)


def kernel(x_prompt, x_sample, cache_k_a, cache_v_a, cache_k_b, cache_v_b, cache_logf_b, p_prompt, p_sample, norm_mix, w_in, b_f, q_norm_a, k_norm_a, q_norm_b, k_norm_b, rel_bias_a, w_out, norm_ffn, w_gate, w_up, w_down, norm_ple, w_ple_gate, w_ple_proj):
    depth = w_in.shape[0]
    b, t, d = x_prompt.shape
    sb, st, _ = x_sample.shape
    n_ha, n_hb = cache_k_a.shape[3], cache_k_b.shape[3]
    d_a, d_b = n_ha * HEAD_DIM, n_hb * HEAD_DIM
    past = cache_k_b.shape[2]
    w_a = cache_k_a.shape[2]
    n_s = sb * st
    assert d_a == d_b and w_a == A_WINDOW and st <= CHUNK and t >= A_WINDOW and n_s % ROW_TILE == 0
    n_groups = n_hb // HEADS_PER_GROUP

    xp = x_prompt
    xs = x_sample.reshape(1, n_s, d)
    outs = [[] for _ in range(10)]
    for i in range(depth):
        wt = w_in[i].T.astype(BF16)
        g_mix = norm_mix[i].reshape(1, d)
        bf_t = jnp.broadcast_to(b_f[i][:, None], (n_hb, ROW_TILE)).astype(F32)
        gains_t = jnp.stack([jnp.broadcast_to(g[i][:, None], (HEAD_DIM, ROW_TILE))
                             for g in (q_norm_a, k_norm_a, q_norm_b, k_norm_b)]).astype(F32)
        tail_w = (w_out[i, :d_a].astype(BF16), w_out[i, d_a:].astype(BF16), norm_ffn[i].reshape(1, d),
                  w_gate[i].astype(BF16), w_up[i].astype(BF16), w_down[i].astype(BF16),
                  norm_ple[i].reshape(1, d), w_ple_gate[i].astype(BF16), w_ple_proj[i].astype(BF16))
        bias_gen = _band_bias_gen(rel_bias_a[i])

        qa_t, qb_t, ka16, kb16, va_t16, vb_t16, ka_t, va_t, kb_t, vb_t, lf_t = _proj(
            xp, g_mix, wt, bf_t, gains_t, n_hb, A_WINDOW)
        c = _cumsum_lanes(lf_t.reshape(b * n_hb, t)).reshape(b, n_groups, HEADS_PER_GROUP, t)
        o_a = _band_prompt(ka16, qa_t, va_t16, bias_gen.reshape(n_ha // HEADS_PER_GROUP, HEADS_PER_GROUP, -1))
        o_b = _fox_prompt(kb16, qb_t, vb_t16, c)
        xp = _tail(xp.reshape(b * t, d), o_a.reshape(b * t, d_a), o_b.reshape(b * t, d_b),
                   p_prompt[i].reshape(b * t, -1), *tail_w).reshape(b, t, d)
        outs[0].append(_time_major(ka_t, n_ha))
        outs[1].append(_time_major(va_t, n_ha))
        outs[2].append(_time_major(kb_t, n_hb))
        outs[3].append(_time_major(vb_t, n_hb))
        outs[4].append(lf_t.transpose(0, 2, 1))

        qa_t, qb_t, ka16, kb16, va_t16, vb_t16, ka_t, va_t, kb_t, vb_t, lf_t = _proj(
            xs, g_mix, wt, bf_t, gains_t, n_hb, n_s)
        per_stream = lambda x_t: x_t.reshape(-1, sb, st).transpose(1, 0, 2)
        unslab = lambda s_t: s_t[0].transpose(1, 0, 2).reshape(s_t.shape[2], n_s)
        new_keys = lambda x_t: jnp.pad(per_stream(x_t).astype(BF16), ((0, 0), (0, 0), (0, SAMPLE_KEY_PAD - st)))
        q_rows = lambda s_t: unslab(s_t).T.reshape(sb, st, -1)
        lf_s = per_stream(lf_t[0])
        lf_all = jnp.concatenate([cache_logf_b[i].astype(F32).transpose(0, 2, 1), lf_s], axis=2)
        t_all = past + SAMPLE_KEY_PAD
        lf_all = jnp.pad(lf_all, ((0, 0), (0, 0), (0, t_all - past - st))).reshape(sb * n_hb, t_all)
        c_all = _cumsum_lanes(lf_all).reshape(sb, n_hb, t_all)
        c_q = c_all[:, :, past:past + st].reshape(sb, n_hb * st, 1)
        bias_s = _band_bias_sample(bias_gen, st, w_a + st)
        o_a = _band_sample(q_rows(qa_t), _feature_major(cache_k_a[i]), _feature_major(cache_v_a[i]),
                           new_keys(ka_t[0]), new_keys(va_t[0]), bias_s.reshape(n_ha * st, BAND_W))
        o_b = _fox_sample(q_rows(qb_t), _feature_major(cache_k_b[i]), _feature_major(cache_v_b[i]),
                          new_keys(kb_t[0]), new_keys(vb_t[0]), c_all, c_q)
        xs = _tail(xs.reshape(n_s, d), o_a.reshape(n_s, d_a), o_b.reshape(n_s, d_b),
                   p_sample[i].reshape(n_s, -1), *tail_w).reshape(1, n_s, d)
        outs[5].append(_time_major(per_stream(ka_t[0]), n_ha))
        outs[6].append(_time_major(per_stream(va_t[0]), n_ha))
        outs[7].append(_time_major(per_stream(kb_t[0]), n_hb))
        outs[8].append(_time_major(per_stream(vb_t[0]), n_hb))
        outs[9].append(lf_s.transpose(0, 2, 1))

    return (xp, xs.reshape(sb, st, d)) + tuple(jnp.stack(o) for o in outs)
```

```python
import functools

import jax
import jax.numpy as jnp
from jax import lax
from jax.experimental import pallas as pl
from jax.experimental.pallas import tpu as pltpu

F32 = jnp.float32
BF16 = jnp.bfloat16

CHUNK = 64
HEAD_DIM = 64
A_LEFT_CHUNKS = 8
A_WINDOW = A_LEFT_CHUNKS * CHUNK
REL_CLIP = 128
ATTN_SCALE = HEAD_DIM ** -0.5
LOG2E = 1.4426950408889634
Q_SCALE = ATTN_SCALE * LOG2E
NEG_INF = -1e30
EPS = 1e-6

LANES = 128
HEADS_PER_GROUP = LANES // HEAD_DIM
VMEM_LIMIT = 56 * 1024 * 1024

ROW_TILE = 512
PROJ_TILE = 1024
TIME_BLOCK = 128
BAND_Q = 2 * CHUNK
BAND_PAIR = 2
BAND_W = A_WINDOW + BAND_Q
FOX_T = 512
FOX_H = FOX_T // 2
FF_CHUNK = 256
ONES_ROWS = 16
BIAS_ROWS = 16
N_SPLIT = 3
SAMPLE_KEY_PAD = 128
FOX_S_KBLOCK = 2048
FOX_S_SUB = 512


def _params(*sem):
    return pltpu.CompilerParams(dimension_semantics=sem, vmem_limit_bytes=VMEM_LIMIT)


def _const_spec(shape):
    zeros = (0,) * len(shape)
    return pl.BlockSpec(shape, lambda *_: zeros, pipeline_mode=pl.Buffered(1))


def _rms(x, gain):
    ms = jnp.mean(x * x, axis=-1, keepdims=True)
    return x * lax.rsqrt(ms + EPS) * gain


def _dot(a, b):
    return jnp.dot(a, b, preferred_element_type=F32)


def _dot_nt(a, b):
    return lax.dot_general(a, b, (((1,), (1,)), ((), ())), preferred_element_type=F32)


def _proj_kernel(x_ref, g_ref, wt_ref, bf_ref, gains_ref,
                 qa_ref, qb_ref, ka16_ref, kb16_ref, va16_ref, vb16_ref,
                 ka_ref, va_ref, kb_ref, vb_ref, lf_ref, *, d_seg, n_fh, sub):
    tm = x_ref.shape[1]
    n_heads = d_seg // HEAD_DIM
    slabs = sub // TIME_BLOCK
    kept_from = tm - ka_ref.shape[2]

    def head_norm(y, gi):
        rows = []
        for hd in range(n_heads):
            yh = y[hd * HEAD_DIM:(hd + 1) * HEAD_DIM, :]
            ms = jnp.sum(yh * yh, axis=0, keepdims=True) * (1.0 / HEAD_DIM)
            rows.append(yh * lax.rsqrt(ms + EPS) * gains_ref[gi])
        return jnp.concatenate(rows, axis=0)

    for u in range(tm // sub):
        cols = slice(u * sub, (u + 1) * sub)
        h = _rms(x_ref[0, cols, :], g_ref[...]).astype(BF16)

        def seg(s, extra=0):
            return _dot_nt(wt_ref[s * d_seg:(s + 1) * d_seg + extra, :], h)

        def put_slabs(ref, y):
            y = y.astype(BF16)
            for v in range(slabs):
                ref[0, u * slabs + v] = y[:, v * TIME_BLOCK:(v + 1) * TIME_BLOCK]

        put_slabs(qa_ref, head_norm(seg(0), 0) * Q_SCALE)
        ka = head_norm(seg(1), 1)
        ka16_ref[0, cols, :] = ka.T.astype(BF16)
        va = seg(2)
        put_slabs(va16_ref, va)
        if (u + 1) * sub > kept_from:
            kept_cols = slice(u * sub - kept_from, (u + 1) * sub - kept_from)
            ka_ref[0, :, kept_cols] = ka
            va_ref[0, :, kept_cols] = va
        put_slabs(qb_ref, head_norm(seg(3), 2) * Q_SCALE)
        kb = head_norm(seg(4), 3)
        kb_ref[0, :, cols] = kb
        kb16_ref[0, cols, :] = kb.T.astype(BF16)
        tail = seg(5, n_fh)
        vb = tail[:d_seg]
        vb_ref[0, :, cols] = vb
        put_slabs(vb16_ref, vb)
        z = tail[d_seg:] + bf_ref[...]
        lf_ref[0, :, cols] = jnp.minimum(z, 0.0) - jnp.log1p(jnp.exp(-jnp.abs(z)))


def _proj(x, g, wt, bf_t, gains_t, n_fh, keep):
    b, t, d = x.shape
    d_seg = (wt.shape[0] - n_fh) // 6
    tm, sub = min(PROJ_TILE, t), ROW_TILE
    assert t % tm == 0 and tm % sub == 0 and sub % TIME_BLOCK == 0 and keep <= tm and keep % sub == 0
    nt, n_slabs = t // tm, tm // TIME_BLOCK
    feat = lambda rows: pl.BlockSpec((1, rows, tm), lambda i, j: (i, 0, j))
    slab = pl.BlockSpec((1, n_slabs, d_seg, TIME_BLOCK), lambda i, j: (i, j, 0, 0))
    kept = pl.BlockSpec((1, d_seg, keep), lambda i, j: (i, 0, 0))
    std = pl.BlockSpec((1, tm, d_seg), lambda i, j: (i, j, 0))
    slab_out = jax.ShapeDtypeStruct((b, t // TIME_BLOCK, d_seg, TIME_BLOCK), BF16)
    std_out = jax.ShapeDtypeStruct((b, t, d_seg), BF16)
    feat_out = jax.ShapeDtypeStruct((b, d_seg, t), F32)
    kept_out = jax.ShapeDtypeStruct((b, d_seg, keep), F32)
    return pl.pallas_call(
        functools.partial(_proj_kernel, d_seg=d_seg, n_fh=n_fh, sub=sub),
        grid=(b, nt),
        in_specs=[pl.BlockSpec((1, tm, d), lambda i, j: (i, j, 0)), _const_spec(g.shape), _const_spec(wt.shape),
                  _const_spec(bf_t.shape), _const_spec(gains_t.shape)],
        out_specs=[slab, slab, std, std, slab, slab, kept, kept, feat(d_seg), feat(d_seg), feat(n_fh)],
        out_shape=[slab_out, slab_out, std_out, std_out, slab_out, slab_out,
                   kept_out, kept_out, feat_out, feat_out, jax.ShapeDtypeStruct((b, n_fh, t), F32)],
        compiler_params=_params("parallel", "arbitrary"),
        name="proj",
    )(x, g, wt, bf_t, gains_t)


def _cumsum_kernel(x_ref, u_ref, c_ref, *, n_blocks):
    u = u_ref[...]
    off = jnp.zeros((x_ref.shape[0], 1), F32)
    for j in range(n_blocks):
        x = x_ref[:, j * LANES:(j + 1) * LANES]
        hi = x.astype(BF16)
        r1 = x - hi.astype(F32)
        mid = r1.astype(BF16)
        lo = (r1 - mid.astype(F32)).astype(BF16)
        c = (_dot(hi, u) + _dot(mid, u) + _dot(lo, u)) + off
        c_ref[:, j * LANES:(j + 1) * LANES] = c
        off = c[:, LANES - 1:LANES]


def _cumsum_lanes(x):
    g, t = x.shape
    assert t % LANES == 0
    idx = jnp.arange(LANES)
    u = (idx[:, None] <= idx[None, :]).astype(BF16)
    return pl.pallas_call(
        functools.partial(_cumsum_kernel, n_blocks=t // LANES),
        out_shape=jax.ShapeDtypeStruct((g, t), F32),
        compiler_params=pltpu.CompilerParams(vmem_limit_bytes=VMEM_LIMIT),
        name="cumsum",
    )(x, u)


def _pair_mask(tq):
    shape = (LANES, HEADS_PER_GROUP * tq)
    row_head = lax.broadcasted_iota(jnp.int32, shape, 0) // HEAD_DIM
    col_head = lax.broadcasted_iota(jnp.int32, shape, 1) // tq
    return (row_head == col_head).astype(F32).astype(BF16)


def _pair_diag(o2, tq):
    return jnp.concatenate([o2[h * HEAD_DIM:(h + 1) * HEAD_DIM, h * tq:(h + 1) * tq]
                            for h in range(HEADS_PER_GROUP)], axis=0)


def _v_dot_p(v_ref, blk0, p, rows=slice(None)):
    keys = p.shape[0]
    v = jnp.concatenate([v_ref[0, blk0 + u, rows, :] for u in range(keys // TIME_BLOCK)], axis=1)
    return _dot(jnp.concatenate([v, jnp.ones((ONES_ROWS, keys), BF16)], axis=0), p)


def _q_tile(q_ref, blk0, n_blk):
    parts = [q_ref[0, blk0 + u] for u in range(n_blk)]
    return parts[0] if n_blk == 1 else jnp.concatenate(parts, axis=1)


def _band_kernel(k_ref, qt_ref, vt_ref, gen_ref, o_ref, s_ref, bias_ref, *, n_tiles):
    head_mask = _pair_mask(BAND_Q)
    q_blks = BAND_Q // TIME_BLOCK
    period = gen_ref.shape[2]

    @pl.when(pl.program_id(1) == 0)
    def _():
        j = lax.broadcasted_iota(jnp.int32, (BAND_W, BAND_Q), 0)
        r = lax.broadcasted_iota(jnp.int32, (BAND_W, BAND_Q), 1)
        gap = j // CHUNK - r // CHUNK
        in_band = (gap >= 0) & (gap <= A_LEFT_CHUNKS)
        for h in range(HEADS_PER_GROUP):
            rows = jnp.broadcast_to(gen_ref[0, h:h + 1, :], (BAND_W, period))
            toeplitz = pltpu.roll(rows, period - BAND_W + 1, 1, stride=1, stride_axis=0)[:, :BAND_Q]
            bias_ref[:, h * BAND_Q:(h + 1) * BAND_Q] = jnp.where(in_band, toeplitz, NEG_INF)

    def window(i):
        k0 = max(i * BAND_Q - A_WINDOW, 0)
        return k0, (i + 1) * BAND_Q - k0

    def scores(n):
        maxes = []
        for a in range(BAND_PAIR):
            i = n * BAND_PAIR + a
            k0, width = window(i)
            q = _q_tile(qt_ref, i * q_blks, q_blks)
            q2 = jnp.concatenate([q] * HEADS_PER_GROUP, axis=1) * head_mask
            s = _dot(k_ref[0, k0:k0 + width, :], q2) + bias_ref[BAND_W - width:, :]
            s_ref[n % 2, a, 0:width, :] = s
            maxes.append(jnp.max(s, axis=0, keepdims=True))
        return maxes

    def finish(n, maxes):
        outs = []
        for a in range(BAND_PAIR):
            i = n * BAND_PAIR + a
            k0, width = window(i)
            p = jnp.exp2(s_ref[n % 2, a, 0:width, :] - maxes[a])
            o2 = _v_dot_p(vt_ref, k0 // TIME_BLOCK, p.astype(BF16))
            outs.append(_pair_diag(o2[:LANES] / o2[LANES:LANES + 1], BAND_Q))
        rows = BAND_PAIR * BAND_Q
        o_ref[0, n * rows:(n + 1) * rows, :] = jnp.concatenate(outs, axis=1).T.astype(o_ref.dtype)

    n_steps = n_tiles // BAND_PAIR
    maxes = scores(0)
    for n in range(n_steps):
        nxt = scores(n + 1) if n + 1 < n_steps else None
        finish(n, maxes)
        maxes = nxt


def _band_prompt(k, q_t, v_t, gen):
    b, t, d = k.shape
    n_groups = d // LANES
    assert t % (BAND_PAIR * BAND_Q) == 0 and BAND_Q % TIME_BLOCK == 0 and A_WINDOW % TIME_BLOCK == 0
    slab = pl.BlockSpec((1, t // TIME_BLOCK, LANES, TIME_BLOCK), lambda j, i: (i, 0, j, 0))
    std = pl.BlockSpec((1, t, LANES), lambda j, i: (i, 0, j))
    return pl.pallas_call(
        functools.partial(_band_kernel, n_tiles=t // BAND_Q),
        grid=(n_groups, b),
        in_specs=[std, slab, slab,
                  pl.BlockSpec((1, HEADS_PER_GROUP, gen.shape[2]), lambda j, i: (j, 0, 0))],
        out_specs=std,
        out_shape=jax.ShapeDtypeStruct((b, t, d), BF16),
        scratch_shapes=[pltpu.VMEM((2, BAND_PAIR, BAND_W, HEADS_PER_GROUP * BAND_Q), F32),
                        pltpu.VMEM((BAND_W, HEADS_PER_GROUP * BAND_Q), F32)],
        compiler_params=_params("parallel", "arbitrary"),
        name="band_prompt",
    )(k, q_t, v_t, gen)


def _fox_kernel(k_ref, qt_ref, vt_ref, c_ref, o_ref, s_ref, kaug_ref, *, n_tiles):
    n_halves = FOX_T // FOX_H
    n_cb = n_halves * HEADS_PER_GROUP
    blks = FOX_T // TIME_BLOCK
    wide = n_cb * FOX_H
    col_blocks = [slice(c * FOX_H, (c + 1) * FOX_H) for c in range(n_cb)]
    head_rows = [slice(h * HEAD_DIM, (h + 1) * HEAD_DIM) for h in range(HEADS_PER_GROUP)]
    row_head = lax.broadcasted_iota(jnp.int32, (LANES, wide), 0) // HEAD_DIM
    col_head = (lax.broadcasted_iota(jnp.int32, (1, wide), 1) // FOX_H) % HEADS_PER_GROUP
    head_mask = (row_head == col_head).astype(F32).astype(BF16)
    key_pos = lax.broadcasted_iota(jnp.int32, (FOX_H, wide), 0)
    col = lax.broadcasted_iota(jnp.int32, (FOX_H, wide), 1)
    causal_lo = (key_pos <= col % FOX_H) | (col >= HEADS_PER_GROUP * FOX_H)
    causal_hi = causal_lo[:, :HEADS_PER_GROUP * FOX_H]

    ones_row0 = N_SPLIT * HEADS_PER_GROUP

    def split3(x):
        hi = x.astype(BF16).astype(F32)
        mid = (x - hi).astype(BF16).astype(F32)
        return hi, mid, (x - hi - mid).astype(BF16).astype(F32)

    def bias_rows(values, width):
        row = lax.broadcasted_iota(jnp.int32, (BIAS_ROWS, width), 0)
        top = jnp.zeros((BIAS_ROWS, width), F32)
        for r, v in enumerate(values):
            top = jnp.where(row == r, v, top)
        return jnp.concatenate([top, jnp.zeros((LANES - BIAS_ROWS, width), F32)], axis=0)

    for j in range(n_tiles):
        parts = split3(c_ref[0, 0, :, j * FOX_T:(j + 1) * FOX_T] * (-LOG2E))
        values = [parts[r // HEADS_PER_GROUP][r % HEADS_PER_GROUP:r % HEADS_PER_GROUP + 1] for r in range(ones_row0)]
        values += [jnp.ones((1, FOX_T), F32)] * N_SPLIT
        kaug_ref[j] = bias_rows(values, FOX_T).T.astype(BF16)

    q2_cache = {}

    def keys_of(c, diag):
        return FOX_H if diag and c < HEADS_PER_GROUP else FOX_T

    def scores(n, i, j):
        if i not in q2_cache:
            q = _q_tile(qt_ref, i * blks, blks)
            q = jnp.concatenate([q[:, (c // HEADS_PER_GROUP) * FOX_H:(c // HEADS_PER_GROUP + 1) * FOX_H]
                                 for c in range(n_cb)], axis=1)
            c0 = split3(c_ref[0, 0, :, i * FOX_T:i * FOX_T + 1] * LOG2E)
            values = [(col_head == r % HEADS_PER_GROUP).astype(F32) for r in range(ones_row0)]
            values += [jnp.where(col_head == 0, part[0:1], part[1:2]) for part in c0]
            q2_cache.clear()
            q2_cache[i] = jnp.concatenate([q * head_mask, bias_rows(values, wide).astype(BF16)], axis=0)
        q2 = q2_cache[i]
        keys = jnp.concatenate([k_ref[0, j * FOX_T:(j + 1) * FOX_T, :], kaug_ref[j]], axis=1)
        if j < i:
            s = _dot(keys, q2)
            s_ref[n % 2] = s
            return [jnp.max(s[:, cb], axis=0, keepdims=True) for cb in col_blocks]
        upper = HEADS_PER_GROUP * FOX_H
        s_lo = jnp.where(causal_lo, _dot(keys[:FOX_H], q2), NEG_INF)
        s_hi = jnp.where(causal_hi, _dot(keys[FOX_H:], q2[:, upper:]), NEG_INF)
        s_ref[n % 2, 0:FOX_H, :] = s_lo
        s_ref[n % 2, FOX_H:, upper:] = s_hi
        maxes = [jnp.max(s_lo[:, cb], axis=0, keepdims=True) for cb in col_blocks]
        for c in range(HEADS_PER_GROUP, n_cb):
            hi_max = jnp.max(s_hi[:, col_blocks[c - HEADS_PER_GROUP]], axis=0, keepdims=True)
            maxes[c] = jnp.maximum(maxes[c], hi_max)
        return maxes

    def update(n, i, j, maxes, carry):
        new = []
        for c, (m, acc) in enumerate(carry):
            nk = keys_of(c, j == i)
            m_new = jnp.maximum(m, maxes[c])
            p = jnp.exp2(s_ref[n % 2, 0:nk, col_blocks[c]] - m_new).astype(BF16)
            pv = _v_dot_p(vt_ref, j * blks, p, head_rows[c % HEADS_PER_GROUP])
            new.append((m_new, jnp.exp2(m - m_new) * acc + pv))
        return new

    pairs = [(i, j) for i in range(n_tiles) for j in range(i + 1)]
    maxes = scores(0, *pairs[0])
    carry = None
    for n, (i, j) in enumerate(pairs):
        nxt = scores(n + 1, *pairs[n + 1]) if n + 1 < len(pairs) else None
        if j == 0:
            carry = [(jnp.full((1, FOX_H), NEG_INF, F32), jnp.zeros((HEAD_DIM + ONES_ROWS, FOX_H), F32))
                     for _ in range(n_cb)]
        carry = update(n, i, j, maxes, carry)
        maxes = nxt
        if j == i:
            accs = [jnp.concatenate([carry[HEADS_PER_GROUP * half + h][1] for half in range(n_halves)], axis=1)
                    for h in range(HEADS_PER_GROUP)]
            out = jnp.concatenate([acc[:HEAD_DIM] / acc[HEAD_DIM:HEAD_DIM + 1] for acc in accs], axis=0)
            o_ref[0, i * FOX_T:(i + 1) * FOX_T, :] = out.T.astype(o_ref.dtype)


def _fox_prompt(k, q_t, v_t, c_rows):
    b, t, d = k.shape
    n_groups = d // LANES
    assert t % FOX_T == 0 and FOX_T % TIME_BLOCK == 0 and (N_SPLIT + 1) * HEADS_PER_GROUP <= BIAS_ROWS
    slab = pl.BlockSpec((1, t // TIME_BLOCK, LANES, TIME_BLOCK), lambda i, j: (i, 0, j, 0))
    std = pl.BlockSpec((1, t, LANES), lambda i, j: (i, 0, j))
    return pl.pallas_call(
        functools.partial(_fox_kernel, n_tiles=t // FOX_T),
        grid=(b, n_groups),
        in_specs=[std, slab, slab,
                  pl.BlockSpec((1, 1, HEADS_PER_GROUP, t), lambda i, j: (i, j, 0, 0))],
        out_specs=std,
        out_shape=jax.ShapeDtypeStruct((b, t, d), BF16),
        scratch_shapes=[pltpu.VMEM((2, FOX_T, HEADS_PER_GROUP * FOX_T), F32),
                        pltpu.VMEM((t // FOX_T, FOX_T, LANES), BF16)],
        compiler_params=_params("parallel", "parallel"),
        name="fox_prompt",
    )(k, q_t, v_t, c_rows)


def _stack_heads(q, n_heads):
    lane_head = lax.broadcasted_iota(jnp.int32, (1, q.shape[1]), 1) // HEAD_DIM
    return jnp.concatenate([q * (lane_head == h).astype(q.dtype) for h in range(n_heads)], axis=0)


def _unstack_heads(o, n_heads):
    t = o.shape[0] // n_heads
    lane_head = lax.broadcasted_iota(jnp.int32, (1, o.shape[1]), 1) // HEAD_DIM
    out = o[0:t]
    for h in range(1, n_heads):
        out = jnp.where(lane_head == h, o[h * t:(h + 1) * t], out)
    return out


def _softmax_step(carry, s, v_t):
    m, l, acc = carry
    m_new = jnp.maximum(m, jnp.max(s, axis=-1, keepdims=True))
    alpha = jnp.exp2(m - m_new)
    p = jnp.exp2(s - m_new)
    l = alpha * l + jnp.sum(p, axis=-1, keepdims=True)
    acc = alpha * acc + _dot_nt(p.astype(BF16), v_t)
    return m_new, l, acc


def _band_sample_kernel(q_ref, kc_ref, vc_ref, kn_ref, vn_ref, bias_ref, o_ref, *, n_heads):
    w = kc_ref.shape[2]
    qs = _stack_heads(q_ref[0], n_heads)
    s_c = _dot(qs, kc_ref[0].astype(BF16)) + bias_ref[:, 0:w]
    s_n = _dot(qs, kn_ref[0]) + bias_ref[:, w:w + SAMPLE_KEY_PAD]
    m = jnp.maximum(jnp.max(s_c, axis=-1, keepdims=True), jnp.max(s_n, axis=-1, keepdims=True))
    p_c = jnp.exp2(s_c - m)
    p_n = jnp.exp2(s_n - m)
    l = jnp.sum(p_c, axis=-1, keepdims=True) + jnp.sum(p_n, axis=-1, keepdims=True)
    o = (_dot_nt(p_c.astype(BF16), vc_ref[0].astype(BF16)) + _dot_nt(p_n.astype(BF16), vn_ref[0])) / l
    o_ref[0] = _unstack_heads(o, n_heads).astype(o_ref.dtype)


def _band_sample(q, k_cache_t, v_cache_t, k_new_t, v_new_t, bias):
    b, t, d = q.shape
    w = k_cache_t.shape[2]
    n_heads = d // HEAD_DIM
    feat = lambda cols: pl.BlockSpec((1, d, cols), lambda i: (i, 0, 0))
    return pl.pallas_call(
        functools.partial(_band_sample_kernel, n_heads=n_heads),
        grid=(b,),
        in_specs=[pl.BlockSpec((1, t, d), lambda i: (i, 0, 0)), feat(w), feat(w),
                  feat(SAMPLE_KEY_PAD), feat(SAMPLE_KEY_PAD), _const_spec(bias.shape)],
        out_specs=pl.BlockSpec((1, t, d), lambda i: (i, 0, 0)),
        out_shape=jax.ShapeDtypeStruct((b, t, d), BF16),
        compiler_params=_params("parallel"),
        name="band_sample",
    )(q, k_cache_t, v_cache_t, k_new_t, v_new_t, bias)


def _fox_sample_kernel(q_ref, kc_ref, vc_ref, kn_ref, vn_ref, ckc_ref, ckn_ref, cq_ref, o_ref,
                       qs_ref, m_ref, l_ref, acc_ref, s_ref, *, n_heads, t):
    j = pl.program_id(1)
    rows = n_heads * t
    sub = s_ref.shape[2]
    n_sub = kc_ref.shape[2] // sub

    @pl.when(j == 0)
    def _():
        qs_ref[...] = _stack_heads(q_ref[0], n_heads)
        m_ref[...] = jnp.full(m_ref.shape, NEG_INF, F32)
        l_ref[...] = jnp.zeros(l_ref.shape, F32)
        acc_ref[...] = jnp.zeros(acc_ref.shape, F32)

    def key_cumsum(ck_ref, cols):
        width = cols.stop - cols.start
        return jnp.concatenate([jnp.broadcast_to(ck_ref[0, h:h + 1, cols], (t, width)) for h in range(n_heads)], axis=0)

    def update(s, v_t):
        m, l, acc = _softmax_step((m_ref[...], l_ref[...], acc_ref[...]), s, v_t)
        m_ref[...] = m
        l_ref[...] = l
        acc_ref[...] = acc

    cq = cq_ref[0]

    def scores(u):
        cols = slice(u * sub, (u + 1) * sub)
        s_ref[u % 2] = (_dot(qs_ref[...], kc_ref[0, :, cols].astype(BF16))
                        + (cq - key_cumsum(ckc_ref, cols)) * LOG2E)

    scores(0)
    for u in range(n_sub):
        if u + 1 < n_sub:
            scores(u + 1)
        update(s_ref[u % 2], vc_ref[0, :, u * sub:(u + 1) * sub].astype(BF16))

    @pl.when(j == pl.num_programs(1) - 1)
    def _():
        s_n = _dot(qs_ref[...], kn_ref[0]) + (cq - key_cumsum(ckn_ref, slice(0, SAMPLE_KEY_PAD))) * LOG2E
        q_pos = lax.broadcasted_iota(jnp.int32, (rows, SAMPLE_KEY_PAD), 0) % t
        k_pos = lax.broadcasted_iota(jnp.int32, (rows, SAMPLE_KEY_PAD), 1)
        update(jnp.where(k_pos <= q_pos, s_n, NEG_INF), vn_ref[0])
        o_ref[0] = _unstack_heads(acc_ref[...] / l_ref[...], n_heads).astype(o_ref.dtype)


def _fox_sample(q, k_cache_t, v_cache_t, k_new_t, v_new_t, c_keys, c_q):
    b, t, d = q.shape
    past = k_cache_t.shape[2]
    n_heads = d // HEAD_DIM
    kb = min(FOX_S_KBLOCK, past)
    sub = min(FOX_S_SUB, kb)
    assert past % kb == 0 and kb % sub == 0 and sub % SAMPLE_KEY_PAD == 0 and t <= SAMPLE_KEY_PAD
    rows = n_heads * t
    per_b = lambda shape: pl.BlockSpec((1,) + shape, lambda i, j: (i, 0, 0))
    cache = pl.BlockSpec((1, d, kb), lambda i, j: (i, 0, j))
    return pl.pallas_call(
        functools.partial(_fox_sample_kernel, n_heads=n_heads, t=t),
        grid=(b, past // kb),
        in_specs=[per_b((t, d)), cache, cache, per_b((d, SAMPLE_KEY_PAD)), per_b((d, SAMPLE_KEY_PAD)),
                  pl.BlockSpec((1, n_heads, kb), lambda i, j: (i, 0, j)),
                  pl.BlockSpec((1, n_heads, SAMPLE_KEY_PAD), lambda i, j: (i, 0, past // SAMPLE_KEY_PAD)),
                  per_b((rows, 1))],
        out_specs=per_b((t, d)),
        out_shape=jax.ShapeDtypeStruct((b, t, d), BF16),
        scratch_shapes=[pltpu.VMEM((rows, d), BF16), pltpu.VMEM((rows, 1), F32),
                        pltpu.VMEM((rows, 1), F32), pltpu.VMEM((rows, d), F32),
                        pltpu.VMEM((2, rows, sub), F32)],
        compiler_params=_params("parallel", "arbitrary"),
        name="fox_sample",
    )(q, k_cache_t, v_cache_t, k_new_t, v_new_t, c_keys, c_keys, c_q)


def _tail_kernel(x_ref, oa_ref, ob_ref, p_ref, woa_ref, wob_ref, nf_ref, wg_ref, wu_ref, wd_ref,
                 np_ref, wpg_ref, wpp_ref, y_ref, *, n_ff_chunks):
    x = x_ref[...] + _dot(oa_ref[...], woa_ref[...]) + _dot(ob_ref[...], wob_ref[...])
    h = _rms(x, nf_ref[...]).astype(BF16)
    ffn = None
    for c in range(n_ff_chunks):
        cols = slice(c * FF_CHUNK, (c + 1) * FF_CHUNK)
        g = _dot(h, wg_ref[:, cols])
        u = _dot(h, wu_ref[:, cols])
        part = _dot((g * jax.nn.sigmoid(g) * u).astype(BF16), wd_ref[cols, :])
        ffn = part if ffn is None else ffn + part
    x = x + ffn
    gate = jax.nn.sigmoid(_dot(_rms(x, np_ref[...]).astype(BF16), wpg_ref[...]))
    y_ref[...] = x + _dot(p_ref[...].astype(BF16), wpp_ref[...]) * gate


def _tail(x, oa, ob, p, woa, wob, nf, wg, wu, wd, npl, wpg, wpp):
    n, d = x.shape
    d_ff = wg.shape[1]
    tm = min(ROW_TILE, n)
    assert n % tm == 0 and d_ff % FF_CHUNK == 0
    row = lambda a: pl.BlockSpec((tm, a.shape[1]), lambda i: (i, 0))
    weights = (woa, wob, nf, wg, wu, wd, npl, wpg, wpp)
    return pl.pallas_call(
        functools.partial(_tail_kernel, n_ff_chunks=d_ff // FF_CHUNK),
        grid=(n // tm,),
        in_specs=[row(x), row(oa), row(ob), row(p)] + [_const_spec(w.shape) for w in weights],
        out_specs=row(x),
        out_shape=jax.ShapeDtypeStruct((n, d), F32),
        compiler_params=_params("parallel"),
        name="tail",
    )(x, oa, ob, p, *weights)


def _band_bias_gen(table):
    table = table.astype(F32) * LOG2E
    n = BAND_W + BAND_Q - 1
    clipped = jnp.concatenate([jnp.repeat(table[:, :1], n, axis=1), table, jnp.repeat(table[:, -1:], n, axis=1)],
                              axis=1)
    first = n + REL_CLIP + A_WINDOW - (BAND_W - 1)
    return jnp.concatenate([clipped[:, first:first + n], jnp.zeros((table.shape[0], 1), F32)], axis=1)


def _band_bias_sample(gen, st, n_keys):
    rev = gen[:, ::-1]
    rows = jnp.stack([rev[:, BAND_Q - t:BAND_Q - t + BAND_W] for t in range(st)], axis=1)
    return jnp.where(jnp.arange(BAND_W)[None, None, :] < n_keys, rows, NEG_INF)


def _feature_major(cache):
    b, t, n, hd = cache.shape
    return cache.transpose(0, 2, 3, 1).reshape(b, n * hd, t)


def _time_major(x_t, n_heads):
    b, _, t = x_t.shape
    return x_t.reshape(b, n_heads, HEAD_DIM, t).transpose(0, 3, 1, 2)


def kernel(x_prompt, x_sample, cache_k_a, cache_v_a, cache_k_b, cache_v_b, cache_logf_b, p_prompt, p_sample, norm_mix, w_in, b_f, q_norm_a, k_norm_a, q_norm_b, k_norm_b, rel_bias_a, w_out, norm_ffn, w_gate, w_up, w_down, norm_ple, w_ple_gate, w_ple_proj):
    depth = w_in.shape[0]
    b, t, d = x_prompt.shape
    sb, st, _ = x_sample.shape
    n_ha, n_hb = cache_k_a.shape[3], cache_k_b.shape[3]
    d_a, d_b = n_ha * HEAD_DIM, n_hb * HEAD_DIM
    past = cache_k_b.shape[2]
    w_a = cache_k_a.shape[2]
    n_s = sb * st
    assert d_a == d_b and w_a == A_WINDOW and st <= CHUNK and t >= A_WINDOW and n_s % ROW_TILE == 0
    n_groups = n_hb // HEADS_PER_GROUP

    xp = x_prompt
    xs = x_sample.reshape(1, n_s, d)
    outs = [[] for _ in range(10)]
    for i in range(depth):
        wt = w_in[i].T.astype(BF16)
        g_mix = norm_mix[i].reshape(1, d)
        bf_t = jnp.broadcast_to(b_f[i][:, None], (n_hb, ROW_TILE)).astype(F32)
        gains_t = jnp.stack([jnp.broadcast_to(g[i][:, None], (HEAD_DIM, ROW_TILE))
                             for g in (q_norm_a, k_norm_a, q_norm_b, k_norm_b)]).astype(F32)
        tail_w = (w_out[i, :d_a].astype(BF16), w_out[i, d_a:].astype(BF16), norm_ffn[i].reshape(1, d),
                  w_gate[i].astype(BF16), w_up[i].astype(BF16), w_down[i].astype(BF16),
                  norm_ple[i].reshape(1, d), w_ple_gate[i].astype(BF16), w_ple_proj[i].astype(BF16))
        bias_gen = _band_bias_gen(rel_bias_a[i])

        qa_t, qb_t, ka16, kb16, va_t16, vb_t16, ka_t, va_t, kb_t, vb_t, lf_t = _proj(
            xp, g_mix, wt, bf_t, gains_t, n_hb, A_WINDOW)
        c = _cumsum_lanes(lf_t.reshape(b * n_hb, t)).reshape(b, n_groups, HEADS_PER_GROUP, t)
        o_a = _band_prompt(ka16, qa_t, va_t16, bias_gen.reshape(n_ha // HEADS_PER_GROUP, HEADS_PER_GROUP, -1))
        o_b = _fox_prompt(kb16, qb_t, vb_t16, c)
        xp = _tail(xp.reshape(b * t, d), o_a.reshape(b * t, d_a), o_b.reshape(b * t, d_b),
                   p_prompt[i].reshape(b * t, -1), *tail_w).reshape(b, t, d)
        outs[0].append(_time_major(ka_t, n_ha))
        outs[1].append(_time_major(va_t, n_ha))
        outs[2].append(_time_major(kb_t, n_hb))
        outs[3].append(_time_major(vb_t, n_hb))
        outs[4].append(lf_t.transpose(0, 2, 1))

        qa_t, qb_t, ka16, kb16, va_t16, vb_t16, ka_t, va_t, kb_t, vb_t, lf_t = _proj(
            xs, g_mix, wt, bf_t, gains_t, n_hb, n_s)
        per_stream = lambda x_t: x_t.reshape(-1, sb, st).transpose(1, 0, 2)
        unslab = lambda s_t: s_t[0].transpose(1, 0, 2).reshape(s_t.shape[2], n_s)
        new_keys = lambda x_t: jnp.pad(per_stream(x_t).astype(BF16), ((0, 0), (0, 0), (0, SAMPLE_KEY_PAD - st)))
        q_rows = lambda s_t: unslab(s_t).T.reshape(sb, st, -1)
        lf_s = per_stream(lf_t[0])
        lf_all = jnp.concatenate([cache_logf_b[i].astype(F32).transpose(0, 2, 1), lf_s], axis=2)
        t_all = past + SAMPLE_KEY_PAD
        lf_all = jnp.pad(lf_all, ((0, 0), (0, 0), (0, t_all - past - st))).reshape(sb * n_hb, t_all)
        c_all = _cumsum_lanes(lf_all).reshape(sb, n_hb, t_all)
        c_q = c_all[:, :, past:past + st].reshape(sb, n_hb * st, 1)
        bias_s = _band_bias_sample(bias_gen, st, w_a + st)
        o_a = _band_sample(q_rows(qa_t), _feature_major(cache_k_a[i]), _feature_major(cache_v_a[i]),
                           new_keys(ka_t[0]), new_keys(va_t[0]), bias_s.reshape(n_ha * st, BAND_W))
        o_b = _fox_sample(q_rows(qb_t), _feature_major(cache_k_b[i]), _feature_major(cache_v_b[i]),
                          new_keys(kb_t[0]), new_keys(vb_t[0]), c_all, c_q)
        xs = _tail(xs.reshape(n_s, d), o_a.reshape(n_s, d_a), o_b.reshape(n_s, d_b),
                   p_sample[i].reshape(n_s, -1), *tail_w).reshape(1, n_s, d)
        outs[5].append(_time_major(per_stream(ka_t[0]), n_ha))
        outs[6].append(_time_major(per_stream(va_t[0]), n_ha))
        outs[7].append(_time_major(per_stream(kb_t[0]), n_hb))
        outs[8].append(_time_major(per_stream(vb_t[0]), n_hb))
        outs[9].append(lf_s.transpose(0, 2, 1))

    return (xp, xs.reshape(sb, st, d)) + tuple(jnp.stack(o) for o in outs)
```

```python
import functools

import jax
import jax.numpy as jnp
from jax import lax
from jax.experimental import pallas as pl
from jax.experimental.pallas import tpu as pltpu

F32 = jnp.float32
BF16 = jnp.bfloat16

CHUNK = 64
HEAD_DIM = 64
A_LEFT_CHUNKS = 8
A_WINDOW = A_LEFT_CHUNKS * CHUNK
REL_CLIP = 128
ATTN_SCALE = HEAD_DIM ** -0.5
LOG2E = 1.4426950408889634
Q_SCALE = ATTN_SCALE * LOG2E
NEG_INF = -1e30
EPS = 1e-6

LANES = 128
HEADS_PER_GROUP = LANES // HEAD_DIM
VMEM_LIMIT = 56 * 1024 * 1024

ROW_TILE = 512
PROJ_TILE = 1024
TIME_BLOCK = 128
BAND_Q = 2 * CHUNK
BAND_PAIR = 2
BAND_W = A_WINDOW + BAND_Q
FOX_T = 512
FOX_H = FOX_T // 2
ATTN_GROUPS = 4
FF_CHUNK = 256
ONES_ROWS = 16
BIAS_ROWS = 16
N_SPLIT = 3
SAMPLE_KEY_PAD = 128
FOX_S_KBLOCK = 2048
FOX_S_SUB = 512


def _params(*sem):
    return pltpu.CompilerParams(dimension_semantics=sem, vmem_limit_bytes=VMEM_LIMIT)


def _const_spec(shape):
    zeros = (0,) * len(shape)
    return pl.BlockSpec(shape, lambda *_: zeros, pipeline_mode=pl.Buffered(1))


def _rms(x, gain):
    ms = jnp.mean(x * x, axis=-1, keepdims=True)
    return x * lax.rsqrt(ms + EPS) * gain


def _dot(a, b):
    return jnp.dot(a, b, preferred_element_type=F32)


def _dot_nt(a, b):
    return lax.dot_general(a, b, (((1,), (1,)), ((), ())), preferred_element_type=F32)


def _proj_kernel(x_ref, g_ref, wt_ref, bf_ref, gains_ref,
                 qa_ref, qb_ref, ka16_ref, kb16_ref, va16_ref, vb16_ref,
                 ka_ref, va_ref, kb_ref, vb_ref, lf_ref, *, d_seg, n_fh, sub):
    tm = x_ref.shape[1]
    n_heads = d_seg // HEAD_DIM
    slabs = sub // TIME_BLOCK
    kept_from = tm - ka_ref.shape[2]

    def head_norm(y, gi):
        rows = []
        for hd in range(n_heads):
            yh = y[hd * HEAD_DIM:(hd + 1) * HEAD_DIM, :]
            ms = jnp.sum(yh * yh, axis=0, keepdims=True) * (1.0 / HEAD_DIM)
            rows.append(yh * lax.rsqrt(ms + EPS) * gains_ref[gi])
        return jnp.concatenate(rows, axis=0)

    for u in range(tm // sub):
        cols = slice(u * sub, (u + 1) * sub)
        h = _rms(x_ref[0, cols, :], g_ref[...]).astype(BF16)

        def seg(s, extra=0):
            return _dot_nt(wt_ref[s * d_seg:(s + 1) * d_seg + extra, :], h)

        def put_slabs(ref, y):
            y = y.astype(BF16)
            for v in range(slabs):
                ref[0, u * slabs + v] = y[:, v * TIME_BLOCK:(v + 1) * TIME_BLOCK]

        put_slabs(qa_ref, head_norm(seg(0), 0) * Q_SCALE)
        ka = head_norm(seg(1), 1)
        ka16_ref[0, cols, :] = ka.T.astype(BF16)
        va = seg(2)
        put_slabs(va16_ref, va)
        if (u + 1) * sub > kept_from:
            kept_cols = slice(u * sub - kept_from, (u + 1) * sub - kept_from)
            ka_ref[0, :, kept_cols] = ka
            va_ref[0, :, kept_cols] = va
        put_slabs(qb_ref, head_norm(seg(3), 2) * Q_SCALE)
        kb = head_norm(seg(4), 3)
        kb_ref[0, :, cols] = kb
        kb16_ref[0, cols, :] = kb.T.astype(BF16)
        tail = seg(5, n_fh)
        vb = tail[:d_seg]
        vb_ref[0, :, cols] = vb
        put_slabs(vb16_ref, vb)
        z = tail[d_seg:] + bf_ref[...]
        lf_ref[0, :, cols] = jnp.minimum(z, 0.0) - jnp.log1p(jnp.exp(-jnp.abs(z)))


def _proj(x, g, wt, bf_t, gains_t, n_fh, keep):
    b, t, d = x.shape
    d_seg = (wt.shape[0] - n_fh) // 6
    tm, sub = min(PROJ_TILE, t), ROW_TILE
    assert t % tm == 0 and tm % sub == 0 and sub % TIME_BLOCK == 0 and keep <= tm and keep % sub == 0
    nt, n_slabs = t // tm, tm // TIME_BLOCK
    feat = lambda rows: pl.BlockSpec((1, rows, tm), lambda i, j: (i, 0, j))
    slab = pl.BlockSpec((1, n_slabs, d_seg, TIME_BLOCK), lambda i, j: (i, j, 0, 0))
    kept = pl.BlockSpec((1, d_seg, keep), lambda i, j: (i, 0, 0))
    std = pl.BlockSpec((1, tm, d_seg), lambda i, j: (i, j, 0))
    slab_out = jax.ShapeDtypeStruct((b, t // TIME_BLOCK, d_seg, TIME_BLOCK), BF16)
    std_out = jax.ShapeDtypeStruct((b, t, d_seg), BF16)
    feat_out = jax.ShapeDtypeStruct((b, d_seg, t), F32)
    kept_out = jax.ShapeDtypeStruct((b, d_seg, keep), F32)
    return pl.pallas_call(
        functools.partial(_proj_kernel, d_seg=d_seg, n_fh=n_fh, sub=sub),
        grid=(b, nt),
        in_specs=[pl.BlockSpec((1, tm, d), lambda i, j: (i, j, 0)), _const_spec(g.shape), _const_spec(wt.shape),
                  _const_spec(bf_t.shape), _const_spec(gains_t.shape)],
        out_specs=[slab, slab, std, std, slab, slab, kept, kept, feat(d_seg), feat(d_seg), feat(n_fh)],
        out_shape=[slab_out, slab_out, std_out, std_out, slab_out, slab_out,
                   kept_out, kept_out, feat_out, feat_out, jax.ShapeDtypeStruct((b, n_fh, t), F32)],
        compiler_params=_params("parallel", "arbitrary"),
        name="proj",
    )(x, g, wt, bf_t, gains_t)


def _cumsum_kernel(x_ref, u_ref, c_ref, *, n_blocks):
    u = u_ref[...]
    off = jnp.zeros((x_ref.shape[0], 1), F32)
    for j in range(n_blocks):
        x = x_ref[:, j * LANES:(j + 1) * LANES]
        hi = x.astype(BF16)
        r1 = x - hi.astype(F32)
        mid = r1.astype(BF16)
        lo = (r1 - mid.astype(F32)).astype(BF16)
        c = (_dot(hi, u) + _dot(mid, u) + _dot(lo, u)) + off
        c_ref[:, j * LANES:(j + 1) * LANES] = c
        off = c[:, LANES - 1:LANES]


def _cumsum_lanes(x):
    g, t = x.shape
    assert t % LANES == 0
    idx = jnp.arange(LANES)
    u = (idx[:, None] <= idx[None, :]).astype(BF16)
    return pl.pallas_call(
        functools.partial(_cumsum_kernel, n_blocks=t // LANES),
        out_shape=jax.ShapeDtypeStruct((g, t), F32),
        compiler_params=pltpu.CompilerParams(vmem_limit_bytes=VMEM_LIMIT),
        name="cumsum",
    )(x, u)


def _pair_mask(tq):
    shape = (LANES, HEADS_PER_GROUP * tq)
    row_head = lax.broadcasted_iota(jnp.int32, shape, 0) // HEAD_DIM
    col_head = lax.broadcasted_iota(jnp.int32, shape, 1) // tq
    return (row_head == col_head).astype(F32).astype(BF16)


def _pair_diag(o2, tq):
    return jnp.concatenate([o2[h * HEAD_DIM:(h + 1) * HEAD_DIM, h * tq:(h + 1) * tq]
                            for h in range(HEADS_PER_GROUP)], axis=0)


def _v_dot_p(v_ref, blk0, p, rows=slice(None)):
    keys = p.shape[0]
    v = jnp.concatenate([v_ref[0, blk0 + u, rows, :] for u in range(keys // TIME_BLOCK)], axis=1)
    return _dot(jnp.concatenate([v, jnp.ones((ONES_ROWS, keys), BF16)], axis=0), p)


def _q_tile(q_ref, blk0, n_blk, rows=slice(None)):
    parts = [q_ref[0, blk0 + u, rows, :] for u in range(n_blk)]
    return parts[0] if n_blk == 1 else jnp.concatenate(parts, axis=1)


def _band_kernel(k_ref, qt_ref, vt_ref, gen_ref, o_ref, s_ref, bias_ref, *, n_tiles, n_groups):
    head_mask = _pair_mask(BAND_Q)
    q_blks = BAND_Q // TIME_BLOCK
    period = gen_ref.shape[2]

    @pl.when(pl.program_id(1) == 0)
    def _():
        j = lax.broadcasted_iota(jnp.int32, (BAND_W, BAND_Q), 0)
        r = lax.broadcasted_iota(jnp.int32, (BAND_W, BAND_Q), 1)
        gap = j // CHUNK - r // CHUNK
        in_band = (gap >= 0) & (gap <= A_LEFT_CHUNKS)
        for g in range(n_groups):
            for h in range(HEADS_PER_GROUP):
                rows = jnp.broadcast_to(gen_ref[g, h:h + 1, :], (BAND_W, period))
                toeplitz = pltpu.roll(rows, period - BAND_W + 1, 1, stride=1, stride_axis=0)[:, :BAND_Q]
                bias_ref[g, :, h * BAND_Q:(h + 1) * BAND_Q] = jnp.where(in_band, toeplitz, NEG_INF)

    def window(i):
        k0 = max(i * BAND_Q - A_WINDOW, 0)
        return k0, (i + 1) * BAND_Q - k0

    def scores(slot, g, n):
        lanes = slice(g * LANES, (g + 1) * LANES)
        maxes = []
        for a in range(BAND_PAIR):
            i = n * BAND_PAIR + a
            k0, width = window(i)
            q = _q_tile(qt_ref, i * q_blks, q_blks, lanes)
            q2 = jnp.concatenate([q] * HEADS_PER_GROUP, axis=1) * head_mask
            s = _dot(k_ref[0, k0:k0 + width, lanes], q2) + bias_ref[g, BAND_W - width:, :]
            s_ref[slot % 2, a, 0:width, :] = s
            maxes.append(jnp.max(s, axis=0, keepdims=True))
        return maxes

    def finish(slot, g, n, maxes):
        lanes = slice(g * LANES, (g + 1) * LANES)
        outs = []
        for a in range(BAND_PAIR):
            i = n * BAND_PAIR + a
            k0, width = window(i)
            p = jnp.exp2(s_ref[slot % 2, a, 0:width, :] - maxes[a])
            o2 = _v_dot_p(vt_ref, k0 // TIME_BLOCK, p.astype(BF16), lanes)
            outs.append(_pair_diag(o2[:LANES] / o2[LANES:LANES + 1], BAND_Q))
        rows = BAND_PAIR * BAND_Q
        o_ref[0, n * rows:(n + 1) * rows, lanes] = jnp.concatenate(outs, axis=1).T.astype(o_ref.dtype)

    steps = [(g, n) for g in range(n_groups) for n in range(n_tiles // BAND_PAIR)]
    maxes = scores(0, *steps[0])
    for slot, (g, n) in enumerate(steps):
        nxt = scores(slot + 1, *steps[slot + 1]) if slot + 1 < len(steps) else None
        finish(slot, g, n, maxes)
        maxes = nxt


def _band_prompt(k, q_t, v_t, gen):
    b, t, d = k.shape
    gs = min(ATTN_GROUPS, d // LANES)
    assert t % (BAND_PAIR * BAND_Q) == 0 and BAND_Q % TIME_BLOCK == 0 and A_WINDOW % TIME_BLOCK == 0
    assert d % (gs * LANES) == 0
    slab = pl.BlockSpec((1, t // TIME_BLOCK, gs * LANES, TIME_BLOCK), lambda j, i: (i, 0, j, 0))
    std = pl.BlockSpec((1, t, gs * LANES), lambda j, i: (i, 0, j))
    return pl.pallas_call(
        functools.partial(_band_kernel, n_tiles=t // BAND_Q, n_groups=gs),
        grid=(d // (gs * LANES), b),
        in_specs=[std, slab, slab,
                  pl.BlockSpec((gs, HEADS_PER_GROUP, gen.shape[2]), lambda j, i: (j, 0, 0))],
        out_specs=std,
        out_shape=jax.ShapeDtypeStruct((b, t, d), BF16),
        scratch_shapes=[pltpu.VMEM((2, BAND_PAIR, BAND_W, HEADS_PER_GROUP * BAND_Q), F32),
                        pltpu.VMEM((gs, BAND_W, HEADS_PER_GROUP * BAND_Q), F32)],
        compiler_params=_params("parallel", "arbitrary"),
        name="band_prompt",
    )(k, q_t, v_t, gen)


def _fox_kernel(k_ref, qt_ref, vt_ref, c_ref, o_ref, s_ref, kaug_ref, *, n_tiles, n_groups):
    n_halves = FOX_T // FOX_H
    n_cb = n_halves * HEADS_PER_GROUP
    blks = FOX_T // TIME_BLOCK
    wide = n_cb * FOX_H
    col_blocks = [slice(c * FOX_H, (c + 1) * FOX_H) for c in range(n_cb)]
    row_head = lax.broadcasted_iota(jnp.int32, (LANES, wide), 0) // HEAD_DIM
    col_head = (lax.broadcasted_iota(jnp.int32, (1, wide), 1) // FOX_H) % HEADS_PER_GROUP
    head_mask = (row_head == col_head).astype(F32).astype(BF16)
    key_pos = lax.broadcasted_iota(jnp.int32, (FOX_H, wide), 0)
    col = lax.broadcasted_iota(jnp.int32, (FOX_H, wide), 1)
    causal_lo = (key_pos <= col % FOX_H) | (col >= HEADS_PER_GROUP * FOX_H)
    causal_hi = causal_lo[:, :HEADS_PER_GROUP * FOX_H]

    ones_row0 = N_SPLIT * HEADS_PER_GROUP

    def split3(x):
        hi = x.astype(BF16).astype(F32)
        mid = (x - hi).astype(BF16).astype(F32)
        return hi, mid, (x - hi - mid).astype(BF16).astype(F32)

    def bias_rows(values, width):
        row = lax.broadcasted_iota(jnp.int32, (BIAS_ROWS, width), 0)
        top = jnp.zeros((BIAS_ROWS, width), F32)
        for r, v in enumerate(values):
            top = jnp.where(row == r, v, top)
        return jnp.concatenate([top, jnp.zeros((LANES - BIAS_ROWS, width), F32)], axis=0)

    for g in range(n_groups):
        for j in range(n_tiles):
            parts = split3(c_ref[0, g, :, j * FOX_T:(j + 1) * FOX_T] * (-LOG2E))
            values = [parts[r // HEADS_PER_GROUP][r % HEADS_PER_GROUP:r % HEADS_PER_GROUP + 1]
                      for r in range(ones_row0)]
            values += [jnp.ones((1, FOX_T), F32)] * N_SPLIT
            kaug_ref[g, j] = bias_rows(values, FOX_T).T.astype(BF16)

    q2_cache = {}

    def keys_of(c, diag):
        return FOX_H if diag and c < HEADS_PER_GROUP else FOX_T

    def scores(n, g, i, j):
        lanes = slice(g * LANES, (g + 1) * LANES)
        if (g, i) not in q2_cache:
            q = _q_tile(qt_ref, i * blks, blks, lanes)
            q = jnp.concatenate([q[:, (c // HEADS_PER_GROUP) * FOX_H:(c // HEADS_PER_GROUP + 1) * FOX_H]
                                 for c in range(n_cb)], axis=1)
            c0 = split3(c_ref[0, g, :, i * FOX_T:i * FOX_T + 1] * LOG2E)
            values = [(col_head == r % HEADS_PER_GROUP).astype(F32) for r in range(ones_row0)]
            values += [jnp.where(col_head == 0, part[0:1], part[1:2]) for part in c0]
            q2_cache.clear()
            q2_cache[g, i] = jnp.concatenate([q * head_mask, bias_rows(values, wide).astype(BF16)], axis=0)
        q2 = q2_cache[g, i]
        keys = jnp.concatenate([k_ref[0, j * FOX_T:(j + 1) * FOX_T, lanes], kaug_ref[g, j]], axis=1)
        if j < i:
            s = _dot(keys, q2)
            s_ref[n % 2] = s
            return [jnp.max(s[:, cb], axis=0, keepdims=True) for cb in col_blocks]
        upper = HEADS_PER_GROUP * FOX_H
        s_lo = jnp.where(causal_lo, _dot(keys[:FOX_H], q2), NEG_INF)
        s_hi = jnp.where(causal_hi, _dot(keys[FOX_H:], q2[:, upper:]), NEG_INF)
        s_ref[n % 2, 0:FOX_H, :] = s_lo
        s_ref[n % 2, FOX_H:, upper:] = s_hi
        maxes = [jnp.max(s_lo[:, cb], axis=0, keepdims=True) for cb in col_blocks]
        for c in range(HEADS_PER_GROUP, n_cb):
            hi_max = jnp.max(s_hi[:, col_blocks[c - HEADS_PER_GROUP]], axis=0, keepdims=True)
            maxes[c] = jnp.maximum(maxes[c], hi_max)
        return maxes

    def update(n, g, i, j, maxes, carry):
        new = []
        for c, (m, acc) in enumerate(carry):
            nk = keys_of(c, j == i)
            m_new = jnp.maximum(m, maxes[c])
            p = jnp.exp2(s_ref[n % 2, 0:nk, col_blocks[c]] - m_new).astype(BF16)
            row0 = g * LANES + (c % HEADS_PER_GROUP) * HEAD_DIM
            pv = _v_dot_p(vt_ref, j * blks, p, slice(row0, row0 + HEAD_DIM))
            new.append((m_new, jnp.exp2(m - m_new) * acc + pv))
        return new

    pairs = [(g, i, j) for g in range(n_groups) for i in range(n_tiles) for j in range(i + 1)]
    maxes = scores(0, *pairs[0])
    carry = None
    for n, (g, i, j) in enumerate(pairs):
        nxt = scores(n + 1, *pairs[n + 1]) if n + 1 < len(pairs) else None
        if j == 0:
            carry = [(jnp.full((1, FOX_H), NEG_INF, F32), jnp.zeros((HEAD_DIM + ONES_ROWS, FOX_H), F32))
                     for _ in range(n_cb)]
        carry = update(n, g, i, j, maxes, carry)
        maxes = nxt
        if j == i:
            accs = [jnp.concatenate([carry[HEADS_PER_GROUP * half + h][1] for half in range(n_halves)], axis=1)
                    for h in range(HEADS_PER_GROUP)]
            out = jnp.concatenate([acc[:HEAD_DIM] / acc[HEAD_DIM:HEAD_DIM + 1] for acc in accs], axis=0)
            o_ref[0, i * FOX_T:(i + 1) * FOX_T, g * LANES:(g + 1) * LANES] = out.T.astype(o_ref.dtype)


def _fox_prompt(k, q_t, v_t, c_rows):
    b, t, d = k.shape
    gs = min(ATTN_GROUPS, d // LANES)
    assert t % FOX_T == 0 and FOX_T % TIME_BLOCK == 0 and (N_SPLIT + 1) * HEADS_PER_GROUP <= BIAS_ROWS
    assert d % (gs * LANES) == 0 and FOX_T == 2 * FOX_H
    slab = pl.BlockSpec((1, t // TIME_BLOCK, gs * LANES, TIME_BLOCK), lambda i, j: (i, 0, j, 0))
    std = pl.BlockSpec((1, t, gs * LANES), lambda i, j: (i, 0, j))
    return pl.pallas_call(
        functools.partial(_fox_kernel, n_tiles=t // FOX_T, n_groups=gs),
        grid=(b, d // (gs * LANES)),
        in_specs=[std, slab, slab,
                  pl.BlockSpec((1, gs, HEADS_PER_GROUP, t), lambda i, j: (i, j, 0, 0))],
        out_specs=std,
        out_shape=jax.ShapeDtypeStruct((b, t, d), BF16),
        scratch_shapes=[pltpu.VMEM((2, FOX_T, HEADS_PER_GROUP * FOX_T), F32),
                        pltpu.VMEM((gs, t // FOX_T, FOX_T, LANES), BF16)],
        compiler_params=_params("parallel", "parallel"),
        name="fox_prompt",
    )(k, q_t, v_t, c_rows)


def _stack_heads(q, n_heads):
    lane_head = lax.broadcasted_iota(jnp.int32, (1, q.shape[1]), 1) // HEAD_DIM
    return jnp.concatenate([q * (lane_head == h).astype(q.dtype) for h in range(n_heads)], axis=0)


def _unstack_heads(o, n_heads):
    t = o.shape[0] // n_heads
    lane_head = lax.broadcasted_iota(jnp.int32, (1, o.shape[1]), 1) // HEAD_DIM
    out = o[0:t]
    for h in range(1, n_heads):
        out = jnp.where(lane_head == h, o[h * t:(h + 1) * t], out)
    return out


def _softmax_step(carry, s, v_t):
    m, l, acc = carry
    m_new = jnp.maximum(m, jnp.max(s, axis=-1, keepdims=True))
    alpha = jnp.exp2(m - m_new)
    p = jnp.exp2(s - m_new)
    l = alpha * l + jnp.sum(p, axis=-1, keepdims=True)
    acc = alpha * acc + _dot_nt(p.astype(BF16), v_t)
    return m_new, l, acc


def _band_sample_kernel(q_ref, kc_ref, vc_ref, kn_ref, vn_ref, bias_ref, o_ref, *, n_heads):
    w = kc_ref.shape[2]
    qs = _stack_heads(q_ref[0], n_heads)
    s_c = _dot(qs, kc_ref[0].astype(BF16)) + bias_ref[:, 0:w]
    s_n = _dot(qs, kn_ref[0]) + bias_ref[:, w:w + SAMPLE_KEY_PAD]
    m = jnp.maximum(jnp.max(s_c, axis=-1, keepdims=True), jnp.max(s_n, axis=-1, keepdims=True))
    p_c = jnp.exp2(s_c - m)
    p_n = jnp.exp2(s_n - m)
    l = jnp.sum(p_c, axis=-1, keepdims=True) + jnp.sum(p_n, axis=-1, keepdims=True)
    o = (_dot_nt(p_c.astype(BF16), vc_ref[0].astype(BF16)) + _dot_nt(p_n.astype(BF16), vn_ref[0])) / l
    o_ref[0] = _unstack_heads(o, n_heads).astype(o_ref.dtype)


def _band_sample(q, k_cache_t, v_cache_t, k_new_t, v_new_t, bias):
    b, t, d = q.shape
    w = k_cache_t.shape[2]
    n_heads = d // HEAD_DIM
    feat = lambda cols: pl.BlockSpec((1, d, cols), lambda i: (i, 0, 0))
    return pl.pallas_call(
        functools.partial(_band_sample_kernel, n_heads=n_heads),
        grid=(b,),
        in_specs=[pl.BlockSpec((1, t, d), lambda i: (i, 0, 0)), feat(w), feat(w),
                  feat(SAMPLE_KEY_PAD), feat(SAMPLE_KEY_PAD), _const_spec(bias.shape)],
        out_specs=pl.BlockSpec((1, t, d), lambda i: (i, 0, 0)),
        out_shape=jax.ShapeDtypeStruct((b, t, d), BF16),
        compiler_params=_params("parallel"),
        name="band_sample",
    )(q, k_cache_t, v_cache_t, k_new_t, v_new_t, bias)


def _fox_sample_kernel(q_ref, kc_ref, vc_ref, kn_ref, vn_ref, ckc_ref, ckn_ref, cq_ref, o_ref,
                       qs_ref, m_ref, l_ref, acc_ref, s_ref, *, n_heads, t):
    j = pl.program_id(1)
    rows = n_heads * t
    sub = s_ref.shape[2]
    n_sub = kc_ref.shape[2] // sub

    @pl.when(j == 0)
    def _():
        qs_ref[...] = _stack_heads(q_ref[0], n_heads)
        m_ref[...] = jnp.full(m_ref.shape, NEG_INF, F32)
        l_ref[...] = jnp.zeros(l_ref.shape, F32)
        acc_ref[...] = jnp.zeros(acc_ref.shape, F32)

    def key_cumsum(ck_ref, cols):
        width = cols.stop - cols.start
        return jnp.concatenate([jnp.broadcast_to(ck_ref[0, h:h + 1, cols], (t, width)) for h in range(n_heads)], axis=0)

    def update(s, v_t):
        m, l, acc = _softmax_step((m_ref[...], l_ref[...], acc_ref[...]), s, v_t)
        m_ref[...] = m
        l_ref[...] = l
        acc_ref[...] = acc

    cq = cq_ref[0]

    def scores(u):
        cols = slice(u * sub, (u + 1) * sub)
        s_ref[u % 2] = (_dot(qs_ref[...], kc_ref[0, :, cols].astype(BF16))
                        + (cq - key_cumsum(ckc_ref, cols)) * LOG2E)

    scores(0)
    for u in range(n_sub):
        if u + 1 < n_sub:
            scores(u + 1)
        update(s_ref[u % 2], vc_ref[0, :, u * sub:(u + 1) * sub].astype(BF16))

    @pl.when(j == pl.num_programs(1) - 1)
    def _():
        s_n = _dot(qs_ref[...], kn_ref[0]) + (cq - key_cumsum(ckn_ref, slice(0, SAMPLE_KEY_PAD))) * LOG2E
        q_pos = lax.broadcasted_iota(jnp.int32, (rows, SAMPLE_KEY_PAD), 0) % t
        k_pos = lax.broadcasted_iota(jnp.int32, (rows, SAMPLE_KEY_PAD), 1)
        update(jnp.where(k_pos <= q_pos, s_n, NEG_INF), vn_ref[0])
        o_ref[0] = _unstack_heads(acc_ref[...] / l_ref[...], n_heads).astype(o_ref.dtype)


def _fox_sample(q, k_cache_t, v_cache_t, k_new_t, v_new_t, c_keys, c_q):
    b, t, d = q.shape
    past = k_cache_t.shape[2]
    n_heads = d // HEAD_DIM
    kb = min(FOX_S_KBLOCK, past)
    sub = min(FOX_S_SUB, kb)
    assert past % kb == 0 and kb % sub == 0 and sub % SAMPLE_KEY_PAD == 0 and t <= SAMPLE_KEY_PAD
    rows = n_heads * t
    per_b = lambda shape: pl.BlockSpec((1,) + shape, lambda i, j: (i, 0, 0))
    cache = pl.BlockSpec((1, d, kb), lambda i, j: (i, 0, j))
    return pl.pallas_call(
        functools.partial(_fox_sample_kernel, n_heads=n_heads, t=t),
        grid=(b, past // kb),
        in_specs=[per_b((t, d)), cache, cache, per_b((d, SAMPLE_KEY_PAD)), per_b((d, SAMPLE_KEY_PAD)),
                  pl.BlockSpec((1, n_heads, kb), lambda i, j: (i, 0, j)),
                  pl.BlockSpec((1, n_heads, SAMPLE_KEY_PAD), lambda i, j: (i, 0, past // SAMPLE_KEY_PAD)),
                  per_b((rows, 1))],
        out_specs=per_b((t, d)),
        out_shape=jax.ShapeDtypeStruct((b, t, d), BF16),
        scratch_shapes=[pltpu.VMEM((rows, d), BF16), pltpu.VMEM((rows, 1), F32),
                        pltpu.VMEM((rows, 1), F32), pltpu.VMEM((rows, d), F32),
                        pltpu.VMEM((2, rows, sub), F32)],
        compiler_params=_params("parallel", "arbitrary"),
        name="fox_sample",
    )(q, k_cache_t, v_cache_t, k_new_t, v_new_t, c_keys, c_keys, c_q)


def _tail_kernel(x_ref, oa_ref, ob_ref, p_ref, woa_ref, wob_ref, nf_ref, wg_ref, wu_ref, wd_ref,
                 np_ref, wpg_ref, wpp_ref, y_ref, *, n_ff_chunks):
    x = x_ref[...] + _dot(oa_ref[...], woa_ref[...]) + _dot(ob_ref[...], wob_ref[...])
    h = _rms(x, nf_ref[...]).astype(BF16)
    ffn = None
    for c in range(n_ff_chunks):
        cols = slice(c * FF_CHUNK, (c + 1) * FF_CHUNK)
        g = _dot(h, wg_ref[:, cols])
        u = _dot(h, wu_ref[:, cols])
        part = _dot((g * jax.nn.sigmoid(g) * u).astype(BF16), wd_ref[cols, :])
        ffn = part if ffn is None else ffn + part
    x = x + ffn
    gate = jax.nn.sigmoid(_dot(_rms(x, np_ref[...]).astype(BF16), wpg_ref[...]))
    y_ref[...] = x + _dot(p_ref[...].astype(BF16), wpp_ref[...]) * gate


def _tail(x, oa, ob, p, woa, wob, nf, wg, wu, wd, npl, wpg, wpp):
    n, d = x.shape
    d_ff = wg.shape[1]
    tm = min(ROW_TILE, n)
    assert n % tm == 0 and d_ff % FF_CHUNK == 0
    row = lambda a: pl.BlockSpec((tm, a.shape[1]), lambda i: (i, 0))
    weights = (woa, wob, nf, wg, wu, wd, npl, wpg, wpp)
    return pl.pallas_call(
        functools.partial(_tail_kernel, n_ff_chunks=d_ff // FF_CHUNK),
        grid=(n // tm,),
        in_specs=[row(x), row(oa), row(ob), row(p)] + [_const_spec(w.shape) for w in weights],
        out_specs=row(x),
        out_shape=jax.ShapeDtypeStruct((n, d), F32),
        compiler_params=_params("parallel"),
        name="tail",
    )(x, oa, ob, p, *weights)


def _band_bias_gen(table):
    table = table.astype(F32) * LOG2E
    n = BAND_W + BAND_Q - 1
    clipped = jnp.concatenate([jnp.repeat(table[:, :1], n, axis=1), table, jnp.repeat(table[:, -1:], n, axis=1)],
                              axis=1)
    first = n + REL_CLIP + A_WINDOW - (BAND_W - 1)
    return jnp.concatenate([clipped[:, first:first + n], jnp.zeros((table.shape[0], 1), F32)], axis=1)


def _band_bias_sample(gen, st, n_keys):
    rev = gen[:, ::-1]
    rows = jnp.stack([rev[:, BAND_Q - t:BAND_Q - t + BAND_W] for t in range(st)], axis=1)
    return jnp.where(jnp.arange(BAND_W)[None, None, :] < n_keys, rows, NEG_INF)


def _feature_major(cache):
    b, t, n, hd = cache.shape
    return cache.transpose(0, 2, 3, 1).reshape(b, n * hd, t)


def _time_major(x_t, n_heads):
    b, _, t = x_t.shape
    return x_t.reshape(b, n_heads, HEAD_DIM, t).transpose(0, 3, 1, 2)


def kernel(x_prompt, x_sample, cache_k_a, cache_v_a, cache_k_b, cache_v_b, cache_logf_b, p_prompt, p_sample, norm_mix, w_in, b_f, q_norm_a, k_norm_a, q_norm_b, k_norm_b, rel_bias_a, w_out, norm_ffn, w_gate, w_up, w_down, norm_ple, w_ple_gate, w_ple_proj):
    depth = w_in.shape[0]
    b, t, d = x_prompt.shape
    sb, st, _ = x_sample.shape
    n_ha, n_hb = cache_k_a.shape[3], cache_k_b.shape[3]
    d_a, d_b = n_ha * HEAD_DIM, n_hb * HEAD_DIM
    past = cache_k_b.shape[2]
    w_a = cache_k_a.shape[2]
    n_s = sb * st
    assert d_a == d_b and w_a == A_WINDOW and st <= CHUNK and t >= A_WINDOW and n_s % ROW_TILE == 0
    n_groups = n_hb // HEADS_PER_GROUP

    xp = x_prompt
    xs = x_sample.reshape(1, n_s, d)
    outs = [[] for _ in range(10)]
    for i in range(depth):
        wt = w_in[i].T.astype(BF16)
        g_mix = norm_mix[i].reshape(1, d)
        bf_t = jnp.broadcast_to(b_f[i][:, None], (n_hb, ROW_TILE)).astype(F32)
        gains_t = jnp.stack([jnp.broadcast_to(g[i][:, None], (HEAD_DIM, ROW_TILE))
                             for g in (q_norm_a, k_norm_a, q_norm_b, k_norm_b)]).astype(F32)
        tail_w = (w_out[i, :d_a].astype(BF16), w_out[i, d_a:].astype(BF16), norm_ffn[i].reshape(1, d),
                  w_gate[i].astype(BF16), w_up[i].astype(BF16), w_down[i].astype(BF16),
                  norm_ple[i].reshape(1, d), w_ple_gate[i].astype(BF16), w_ple_proj[i].astype(BF16))
        bias_gen = _band_bias_gen(rel_bias_a[i])

        qa_t, qb_t, ka16, kb16, va_t16, vb_t16, ka_t, va_t, kb_t, vb_t, lf_t = _proj(
            xp, g_mix, wt, bf_t, gains_t, n_hb, A_WINDOW)
        c = _cumsum_lanes(lf_t.reshape(b * n_hb, t)).reshape(b, n_groups, HEADS_PER_GROUP, t)
        o_a = _band_prompt(ka16, qa_t, va_t16, bias_gen.reshape(n_ha // HEADS_PER_GROUP, HEADS_PER_GROUP, -1))
        o_b = _fox_prompt(kb16, qb_t, vb_t16, c)
        xp = _tail(xp.reshape(b * t, d), o_a.reshape(b * t, d_a), o_b.reshape(b * t, d_b),
                   p_prompt[i].reshape(b * t, -1), *tail_w).reshape(b, t, d)
        outs[0].append(_time_major(ka_t, n_ha))
        outs[1].append(_time_major(va_t, n_ha))
        outs[2].append(_time_major(kb_t, n_hb))
        outs[3].append(_time_major(vb_t, n_hb))
        outs[4].append(lf_t.transpose(0, 2, 1))

        qa_t, qb_t, ka16, kb16, va_t16, vb_t16, ka_t, va_t, kb_t, vb_t, lf_t = _proj(
            xs, g_mix, wt, bf_t, gains_t, n_hb, n_s)
        per_stream = lambda x_t: x_t.reshape(-1, sb, st).transpose(1, 0, 2)
        unslab = lambda s_t: s_t[0].transpose(1, 0, 2).reshape(s_t.shape[2], n_s)
        new_keys = lambda x_t: jnp.pad(per_stream(x_t).astype(BF16), ((0, 0), (0, 0), (0, SAMPLE_KEY_PAD - st)))
        q_rows = lambda s_t: unslab(s_t).T.reshape(sb, st, -1)
        lf_s = per_stream(lf_t[0])
        lf_all = jnp.concatenate([cache_logf_b[i].astype(F32).transpose(0, 2, 1), lf_s], axis=2)
        t_all = past + SAMPLE_KEY_PAD
        lf_all = jnp.pad(lf_all, ((0, 0), (0, 0), (0, t_all - past - st))).reshape(sb * n_hb, t_all)
        c_all = _cumsum_lanes(lf_all).reshape(sb, n_hb, t_all)
        c_q = c_all[:, :, past:past + st].reshape(sb, n_hb * st, 1)
        bias_s = _band_bias_sample(bias_gen, st, w_a + st)
        o_a = _band_sample(q_rows(qa_t), _feature_major(cache_k_a[i]), _feature_major(cache_v_a[i]),
                           new_keys(ka_t[0]), new_keys(va_t[0]), bias_s.reshape(n_ha * st, BAND_W))
        o_b = _fox_sample(q_rows(qb_t), _feature_major(cache_k_b[i]), _feature_major(cache_v_b[i]),
                          new_keys(kb_t[0]), new_keys(vb_t[0]), c_all, c_q)
        xs = _tail(xs.reshape(n_s, d), o_a.reshape(n_s, d_a), o_b.reshape(n_s, d_b),
                   p_sample[i].reshape(n_s, -1), *tail_w).reshape(1, n_s, d)
        outs[5].append(_time_major(per_stream(ka_t[0]), n_ha))
        outs[6].append(_time_major(per_stream(va_t[0]), n_ha))
        outs[7].append(_time_major(per_stream(kb_t[0]), n_hb))
        outs[8].append(_time_major(per_stream(vb_t[0]), n_hb))
        outs[9].append(lf_s.transpose(0, 2, 1))

    return (xp, xs.reshape(sb, st, d)) + tuple(jnp.stack(o) for o in outs)
```

```python
import functools

import jax
import jax.numpy as jnp
from jax import lax
from jax.experimental import pallas as pl
from jax.experimental.pallas import tpu as pltpu

F32 = jnp.float32
BF16 = jnp.bfloat16

CHUNK = 64
HEAD_DIM = 64
A_LEFT_CHUNKS = 8
A_WINDOW = A_LEFT_CHUNKS * CHUNK
REL_CLIP = 128
ATTN_SCALE = HEAD_DIM ** -0.5
LOG2E = 1.4426950408889634
Q_SCALE = ATTN_SCALE * LOG2E
NEG_INF = -1e30
EPS = 1e-6

LANES = 128
HEADS_PER_GROUP = LANES // HEAD_DIM
VMEM_LIMIT = 56 * 1024 * 1024

ROW_TILE = 512
PROJ_TILE = 1024
TIME_BLOCK = 128
BAND_Q = 2 * CHUNK
BAND_PAIR = 2
BAND_W = A_WINDOW + BAND_Q
FOX_T = 512
FOX_H = FOX_T // 2
ATTN_GROUPS = 4
FF_CHUNK = 256
ONES_ROWS = 16
BIAS_ROWS = 16
N_SPLIT = 3
SAMPLE_KEY_PAD = 128
FOX_S_KBLOCK = 4096
FOX_S_SUB = 512


def _params(*sem):
    return pltpu.CompilerParams(dimension_semantics=sem, vmem_limit_bytes=VMEM_LIMIT)


def _const_spec(shape):
    zeros = (0,) * len(shape)
    return pl.BlockSpec(shape, lambda *_: zeros, pipeline_mode=pl.Buffered(1))


def _rms(x, gain):
    ms = jnp.mean(x * x, axis=-1, keepdims=True)
    return x * lax.rsqrt(ms + EPS) * gain


def _dot(a, b):
    return jnp.dot(a, b, preferred_element_type=F32)


def _dot_nt(a, b):
    return lax.dot_general(a, b, (((1,), (1,)), ((), ())), preferred_element_type=F32)


def _proj_kernel(x_ref, g_ref, wt_ref, bf_ref, gains_ref,
                 qa_ref, qb_ref, ka16_ref, kb16_ref, va16_ref, vb16_ref,
                 ka_ref, va_ref, kb_ref, vb_ref, lf_ref, *, d_seg, n_fh, sub):
    tm = x_ref.shape[1]
    n_heads = d_seg // HEAD_DIM
    slabs = sub // TIME_BLOCK
    kept_from = tm - ka_ref.shape[2]

    def head_norm(y, gi):
        rows = []
        for hd in range(n_heads):
            yh = y[hd * HEAD_DIM:(hd + 1) * HEAD_DIM, :]
            ms = jnp.sum(yh * yh, axis=0, keepdims=True) * (1.0 / HEAD_DIM)
            rows.append(yh * lax.rsqrt(ms + EPS) * gains_ref[gi])
        return jnp.concatenate(rows, axis=0)

    for u in range(tm // sub):
        cols = slice(u * sub, (u + 1) * sub)
        h = _rms(x_ref[0, cols, :], g_ref[...]).astype(BF16)

        def seg(s, extra=0):
            return _dot_nt(wt_ref[s * d_seg:(s + 1) * d_seg + extra, :], h)

        def put_slabs(ref, y):
            y = y.astype(BF16)
            for v in range(slabs):
                ref[0, u * slabs + v] = y[:, v * TIME_BLOCK:(v + 1) * TIME_BLOCK]

        put_slabs(qa_ref, head_norm(seg(0), 0) * Q_SCALE)
        ka = head_norm(seg(1), 1)
        ka16_ref[0, cols, :] = ka.T.astype(BF16)
        va = seg(2)
        put_slabs(va16_ref, va)
        if (u + 1) * sub > kept_from:
            kept_cols = slice(u * sub - kept_from, (u + 1) * sub - kept_from)
            ka_ref[0, :, kept_cols] = ka
            va_ref[0, :, kept_cols] = va
        put_slabs(qb_ref, head_norm(seg(3), 2) * Q_SCALE)
        kb = head_norm(seg(4), 3)
        kb_ref[0, :, cols] = kb
        kb16_ref[0, cols, :] = kb.T.astype(BF16)
        tail = seg(5, n_fh)
        vb = tail[:d_seg]
        vb_ref[0, :, cols] = vb
        put_slabs(vb16_ref, vb)
        z = tail[d_seg:] + bf_ref[...]
        lf_ref[0, :, cols] = jnp.minimum(z, 0.0) - jnp.log1p(jnp.exp(-jnp.abs(z)))


def _proj(x, g, wt, bf_t, gains_t, n_fh, keep):
    b, t, d = x.shape
    d_seg = (wt.shape[0] - n_fh) // 6
    tm, sub = min(PROJ_TILE, t), ROW_TILE
    assert t % tm == 0 and tm % sub == 0 and sub % TIME_BLOCK == 0 and keep <= tm and keep % sub == 0
    nt, n_slabs = t // tm, tm // TIME_BLOCK
    feat = lambda rows: pl.BlockSpec((1, rows, tm), lambda i, j: (i, 0, j))
    slab = pl.BlockSpec((1, n_slabs, d_seg, TIME_BLOCK), lambda i, j: (i, j, 0, 0))
    kept = pl.BlockSpec((1, d_seg, keep), lambda i, j: (i, 0, 0))
    std = pl.BlockSpec((1, tm, d_seg), lambda i, j: (i, j, 0))
    slab_out = jax.ShapeDtypeStruct((b, t // TIME_BLOCK, d_seg, TIME_BLOCK), BF16)
    std_out = jax.ShapeDtypeStruct((b, t, d_seg), BF16)
    feat_out = jax.ShapeDtypeStruct((b, d_seg, t), F32)
    kept_out = jax.ShapeDtypeStruct((b, d_seg, keep), F32)
    return pl.pallas_call(
        functools.partial(_proj_kernel, d_seg=d_seg, n_fh=n_fh, sub=sub),
        grid=(b, nt),
        in_specs=[pl.BlockSpec((1, tm, d), lambda i, j: (i, j, 0)), _const_spec(g.shape), _const_spec(wt.shape),
                  _const_spec(bf_t.shape), _const_spec(gains_t.shape)],
        out_specs=[slab, slab, std, std, slab, slab, kept, kept, feat(d_seg), feat(d_seg), feat(n_fh)],
        out_shape=[slab_out, slab_out, std_out, std_out, slab_out, slab_out,
                   kept_out, kept_out, feat_out, feat_out, jax.ShapeDtypeStruct((b, n_fh, t), F32)],
        compiler_params=_params("parallel", "arbitrary"),
        name="proj",
    )(x, g, wt, bf_t, gains_t)


def _cumsum_kernel(x_ref, u_ref, c_ref, *, n_blocks):
    u = u_ref[...]
    off = jnp.zeros((x_ref.shape[0], 1), F32)
    for j in range(n_blocks):
        x = x_ref[:, j * LANES:(j + 1) * LANES]
        hi = x.astype(BF16)
        r1 = x - hi.astype(F32)
        mid = r1.astype(BF16)
        lo = (r1 - mid.astype(F32)).astype(BF16)
        c = (_dot(hi, u) + _dot(mid, u) + _dot(lo, u)) + off
        c_ref[:, j * LANES:(j + 1) * LANES] = c
        off = c[:, LANES - 1:LANES]


def _cumsum_lanes(x):
    g, t = x.shape
    assert t % LANES == 0
    idx = jnp.arange(LANES)
    u = (idx[:, None] <= idx[None, :]).astype(BF16)
    return pl.pallas_call(
        functools.partial(_cumsum_kernel, n_blocks=t // LANES),
        out_shape=jax.ShapeDtypeStruct((g, t), F32),
        compiler_params=pltpu.CompilerParams(vmem_limit_bytes=VMEM_LIMIT),
        name="cumsum",
    )(x, u)


def _pair_mask(tq):
    shape = (LANES, HEADS_PER_GROUP * tq)
    row_head = lax.broadcasted_iota(jnp.int32, shape, 0) // HEAD_DIM
    col_head = lax.broadcasted_iota(jnp.int32, shape, 1) // tq
    return (row_head == col_head).astype(F32).astype(BF16)


def _pair_diag(o2, tq):
    return jnp.concatenate([o2[h * HEAD_DIM:(h + 1) * HEAD_DIM, h * tq:(h + 1) * tq]
                            for h in range(HEADS_PER_GROUP)], axis=0)


def _v_dot_p(v_ref, blk0, p, rows=slice(None)):
    keys = p.shape[0]
    v = jnp.concatenate([v_ref[0, blk0 + u, rows, :] for u in range(keys // TIME_BLOCK)], axis=1)
    return _dot(jnp.concatenate([v, jnp.ones((ONES_ROWS, keys), BF16)], axis=0), p)


def _q_tile(q_ref, blk0, n_blk, rows=slice(None)):
    parts = [q_ref[0, blk0 + u, rows, :] for u in range(n_blk)]
    return parts[0] if n_blk == 1 else jnp.concatenate(parts, axis=1)


def _band_kernel(k_ref, qt_ref, vt_ref, gen_ref, o_ref, s_ref, bias_ref, *, n_tiles, n_groups):
    head_mask = _pair_mask(BAND_Q)
    q_blks = BAND_Q // TIME_BLOCK
    period = gen_ref.shape[2]

    @pl.when(pl.program_id(1) == 0)
    def _():
        j = lax.broadcasted_iota(jnp.int32, (BAND_W, BAND_Q), 0)
        r = lax.broadcasted_iota(jnp.int32, (BAND_W, BAND_Q), 1)
        gap = j // CHUNK - r // CHUNK
        in_band = (gap >= 0) & (gap <= A_LEFT_CHUNKS)
        for g in range(n_groups):
            for h in range(HEADS_PER_GROUP):
                rows = jnp.broadcast_to(gen_ref[g, h:h + 1, :], (BAND_W, period))
                toeplitz = pltpu.roll(rows, period - BAND_W + 1, 1, stride=1, stride_axis=0)[:, :BAND_Q]
                bias_ref[g, :, h * BAND_Q:(h + 1) * BAND_Q] = jnp.where(in_band, toeplitz, NEG_INF)

    def window(i):
        k0 = max(i * BAND_Q - A_WINDOW, 0)
        return k0, (i + 1) * BAND_Q - k0

    def scores(slot, g, n):
        lanes = slice(g * LANES, (g + 1) * LANES)
        maxes = []
        for a in range(BAND_PAIR):
            i = n * BAND_PAIR + a
            k0, width = window(i)
            q = _q_tile(qt_ref, i * q_blks, q_blks, lanes)
            q2 = jnp.concatenate([q] * HEADS_PER_GROUP, axis=1) * head_mask
            s = _dot(k_ref[0, k0:k0 + width, lanes], q2) + bias_ref[g, BAND_W - width:, :]
            s_ref[slot % 2, a, 0:width, :] = s
            maxes.append(jnp.max(s, axis=0, keepdims=True))
        return maxes

    def finish(slot, g, n, maxes):
        lanes = slice(g * LANES, (g + 1) * LANES)
        outs = []
        for a in range(BAND_PAIR):
            i = n * BAND_PAIR + a
            k0, width = window(i)
            p = jnp.exp2(s_ref[slot % 2, a, 0:width, :] - maxes[a])
            o2 = _v_dot_p(vt_ref, k0 // TIME_BLOCK, p.astype(BF16), lanes)
            outs.append(_pair_diag(o2[:LANES] / o2[LANES:LANES + 1], BAND_Q))
        rows = BAND_PAIR * BAND_Q
        o_ref[0, n * rows:(n + 1) * rows, lanes] = jnp.concatenate(outs, axis=1).T.astype(o_ref.dtype)

    steps = [(g, n) for g in range(n_groups) for n in range(n_tiles // BAND_PAIR)]
    maxes = scores(0, *steps[0])
    for slot, (g, n) in enumerate(steps):
        nxt = scores(slot + 1, *steps[slot + 1]) if slot + 1 < len(steps) else None
        finish(slot, g, n, maxes)
        maxes = nxt


def _band_prompt(k, q_t, v_t, gen):
    b, t, d = k.shape
    gs = min(ATTN_GROUPS, d // LANES)
    assert t % (BAND_PAIR * BAND_Q) == 0 and BAND_Q % TIME_BLOCK == 0 and A_WINDOW % TIME_BLOCK == 0
    assert d % (gs * LANES) == 0
    slab = pl.BlockSpec((1, t // TIME_BLOCK, gs * LANES, TIME_BLOCK), lambda j, i: (i, 0, j, 0))
    std = pl.BlockSpec((1, t, gs * LANES), lambda j, i: (i, 0, j))
    return pl.pallas_call(
        functools.partial(_band_kernel, n_tiles=t // BAND_Q, n_groups=gs),
        grid=(d // (gs * LANES), b),
        in_specs=[std, slab, slab,
                  pl.BlockSpec((gs, HEADS_PER_GROUP, gen.shape[2]), lambda j, i: (j, 0, 0))],
        out_specs=std,
        out_shape=jax.ShapeDtypeStruct((b, t, d), BF16),
        scratch_shapes=[pltpu.VMEM((2, BAND_PAIR, BAND_W, HEADS_PER_GROUP * BAND_Q), F32),
                        pltpu.VMEM((gs, BAND_W, HEADS_PER_GROUP * BAND_Q), F32)],
        compiler_params=_params("parallel", "arbitrary"),
        name="band_prompt",
    )(k, q_t, v_t, gen)


def _fox_kernel(k_ref, qt_ref, vt_ref, c_ref, o_ref, s_ref, kaug_ref, *, n_tiles, n_groups):
    n_halves = FOX_T // FOX_H
    n_cb = n_halves * HEADS_PER_GROUP
    blks = FOX_T // TIME_BLOCK
    wide = n_cb * FOX_H
    col_blocks = [slice(c * FOX_H, (c + 1) * FOX_H) for c in range(n_cb)]
    row_head = lax.broadcasted_iota(jnp.int32, (LANES, wide), 0) // HEAD_DIM
    col_head = (lax.broadcasted_iota(jnp.int32, (1, wide), 1) // FOX_H) % HEADS_PER_GROUP
    head_mask = (row_head == col_head).astype(F32).astype(BF16)
    key_pos = lax.broadcasted_iota(jnp.int32, (FOX_H, wide), 0)
    col = lax.broadcasted_iota(jnp.int32, (FOX_H, wide), 1)
    causal_lo = (key_pos <= col % FOX_H) | (col >= HEADS_PER_GROUP * FOX_H)
    causal_hi = causal_lo[:, :HEADS_PER_GROUP * FOX_H]

    ones_row0 = N_SPLIT * HEADS_PER_GROUP

    def split3(x):
        hi = x.astype(BF16).astype(F32)
        mid = (x - hi).astype(BF16).astype(F32)
        return hi, mid, (x - hi - mid).astype(BF16).astype(F32)

    def bias_rows(values, width):
        row = lax.broadcasted_iota(jnp.int32, (BIAS_ROWS, width), 0)
        top = jnp.zeros((BIAS_ROWS, width), F32)
        for r, v in enumerate(values):
            top = jnp.where(row == r, v, top)
        return jnp.concatenate([top, jnp.zeros((LANES - BIAS_ROWS, width), F32)], axis=0)

    for g in range(n_groups):
        for j in range(n_tiles):
            parts = split3(c_ref[0, g, :, j * FOX_T:(j + 1) * FOX_T] * (-LOG2E))
            values = [parts[r // HEADS_PER_GROUP][r % HEADS_PER_GROUP:r % HEADS_PER_GROUP + 1]
                      for r in range(ones_row0)]
            values += [jnp.ones((1, FOX_T), F32)] * N_SPLIT
            kaug_ref[g, j] = bias_rows(values, FOX_T).T.astype(BF16)

    q2_cache = {}

    def keys_of(c, diag):
        return FOX_H if diag and c < HEADS_PER_GROUP else FOX_T

    def scores(n, g, i, j):
        lanes = slice(g * LANES, (g + 1) * LANES)
        if (g, i) not in q2_cache:
            q = _q_tile(qt_ref, i * blks, blks, lanes)
            q = jnp.concatenate([q[:, (c // HEADS_PER_GROUP) * FOX_H:(c // HEADS_PER_GROUP + 1) * FOX_H]
                                 for c in range(n_cb)], axis=1)
            c0 = split3(c_ref[0, g, :, i * FOX_T:i * FOX_T + 1] * LOG2E)
            values = [(col_head == r % HEADS_PER_GROUP).astype(F32) for r in range(ones_row0)]
            values += [jnp.where(col_head == 0, part[0:1], part[1:2]) for part in c0]
            q2_cache.clear()
            q2_cache[g, i] = jnp.concatenate([q * head_mask, bias_rows(values, wide).astype(BF16)], axis=0)
        q2 = q2_cache[g, i]
        keys = jnp.concatenate([k_ref[0, j * FOX_T:(j + 1) * FOX_T, lanes], kaug_ref[g, j]], axis=1)
        if j < i:
            s = _dot(keys, q2)
            s_ref[n % 2] = s
            return [jnp.max(s[:, cb], axis=0, keepdims=True) for cb in col_blocks]
        upper = HEADS_PER_GROUP * FOX_H
        s_lo = jnp.where(causal_lo, _dot(keys[:FOX_H], q2), NEG_INF)
        s_hi = jnp.where(causal_hi, _dot(keys[FOX_H:], q2[:, upper:]), NEG_INF)
        s_ref[n % 2, 0:FOX_H, :] = s_lo
        s_ref[n % 2, FOX_H:, upper:] = s_hi
        maxes = [jnp.max(s_lo[:, cb], axis=0, keepdims=True) for cb in col_blocks]
        for c in range(HEADS_PER_GROUP, n_cb):
            hi_max = jnp.max(s_hi[:, col_blocks[c - HEADS_PER_GROUP]], axis=0, keepdims=True)
            maxes[c] = jnp.maximum(maxes[c], hi_max)
        return maxes

    def update(n, g, i, j, maxes, carry):
        new = []
        for c, (m, acc) in enumerate(carry):
            nk = keys_of(c, j == i)
            m_new = jnp.maximum(m, maxes[c])
            p = jnp.exp2(s_ref[n % 2, 0:nk, col_blocks[c]] - m_new).astype(BF16)
            row0 = g * LANES + (c % HEADS_PER_GROUP) * HEAD_DIM
            pv = _v_dot_p(vt_ref, j * blks, p, slice(row0, row0 + HEAD_DIM))
            new.append((m_new, jnp.exp2(m - m_new) * acc + pv))
        return new

    pairs = [(g, i, j) for g in range(n_groups) for i in range(n_tiles) for j in range(i + 1)]
    maxes = scores(0, *pairs[0])
    carry = None
    for n, (g, i, j) in enumerate(pairs):
        nxt = scores(n + 1, *pairs[n + 1]) if n + 1 < len(pairs) else None
        if j == 0:
            carry = [(jnp.full((1, FOX_H), NEG_INF, F32), jnp.zeros((HEAD_DIM + ONES_ROWS, FOX_H), F32))
                     for _ in range(n_cb)]
        carry = update(n, g, i, j, maxes, carry)
        maxes = nxt
        if j == i:
            accs = [jnp.concatenate([carry[HEADS_PER_GROUP * half + h][1] for half in range(n_halves)], axis=1)
                    for h in range(HEADS_PER_GROUP)]
            out = jnp.concatenate([acc[:HEAD_DIM] / acc[HEAD_DIM:HEAD_DIM + 1] for acc in accs], axis=0)
            o_ref[0, i * FOX_T:(i + 1) * FOX_T, g * LANES:(g + 1) * LANES] = out.T.astype(o_ref.dtype)


def _fox_prompt(k, q_t, v_t, c_rows):
    b, t, d = k.shape
    gs = min(ATTN_GROUPS, d // LANES)
    assert t % FOX_T == 0 and FOX_T % TIME_BLOCK == 0 and (N_SPLIT + 1) * HEADS_PER_GROUP <= BIAS_ROWS
    assert d % (gs * LANES) == 0 and FOX_T == 2 * FOX_H
    slab = pl.BlockSpec((1, t // TIME_BLOCK, gs * LANES, TIME_BLOCK), lambda i, j: (i, 0, j, 0))
    std = pl.BlockSpec((1, t, gs * LANES), lambda i, j: (i, 0, j))
    return pl.pallas_call(
        functools.partial(_fox_kernel, n_tiles=t // FOX_T, n_groups=gs),
        grid=(b, d // (gs * LANES)),
        in_specs=[std, slab, slab,
                  pl.BlockSpec((1, gs, HEADS_PER_GROUP, t), lambda i, j: (i, j, 0, 0))],
        out_specs=std,
        out_shape=jax.ShapeDtypeStruct((b, t, d), BF16),
        scratch_shapes=[pltpu.VMEM((2, FOX_T, HEADS_PER_GROUP * FOX_T), F32),
                        pltpu.VMEM((gs, t // FOX_T, FOX_T, LANES), BF16)],
        compiler_params=_params("parallel", "parallel"),
        name="fox_prompt",
    )(k, q_t, v_t, c_rows)


def _stack_heads(q, n_heads):
    lane_head = lax.broadcasted_iota(jnp.int32, (1, q.shape[1]), 1) // HEAD_DIM
    return jnp.concatenate([q * (lane_head == h).astype(q.dtype) for h in range(n_heads)], axis=0)


def _unstack_heads(o, n_heads):
    t = o.shape[0] // n_heads
    lane_head = lax.broadcasted_iota(jnp.int32, (1, o.shape[1]), 1) // HEAD_DIM
    out = o[0:t]
    for h in range(1, n_heads):
        out = jnp.where(lane_head == h, o[h * t:(h + 1) * t], out)
    return out


def _softmax_step(carry, s, v_t):
    m, l, acc = carry
    m_new = jnp.maximum(m, jnp.max(s, axis=-1, keepdims=True))
    alpha = jnp.exp2(m - m_new)
    p = jnp.exp2(s - m_new)
    l = alpha * l + jnp.sum(p, axis=-1, keepdims=True)
    acc = alpha * acc + _dot_nt(p.astype(BF16), v_t)
    return m_new, l, acc


def _band_sample_kernel(q_ref, kc_ref, vc_ref, kn_ref, vn_ref, bias_ref, bias_new_ref, o_ref, *, n_heads):
    w = kc_ref.shape[2]
    qs = _stack_heads(q_ref[0], n_heads)
    s_c = _dot(qs, kc_ref[0].astype(BF16)) + bias_ref[:, 0:w]
    s_n = _dot(qs, kn_ref[0].astype(BF16)) + bias_new_ref[0]
    m = jnp.maximum(jnp.max(s_c, axis=-1, keepdims=True), jnp.max(s_n, axis=-1, keepdims=True))
    p_c = jnp.exp2(s_c - m)
    p_n = jnp.exp2(s_n - m)
    l = jnp.sum(p_c, axis=-1, keepdims=True) + jnp.sum(p_n, axis=-1, keepdims=True)
    o = (_dot_nt(p_c.astype(BF16), vc_ref[0].astype(BF16)) + _dot_nt(p_n.astype(BF16), vn_ref[0].astype(BF16))) / l
    o_ref[0] = _unstack_heads(o, n_heads).astype(o_ref.dtype)


def _new_key_spec(d, t):
    per = SAMPLE_KEY_PAD // t
    return pl.BlockSpec((1, d, SAMPLE_KEY_PAD), lambda i, *_: (0, 0, i // per))


def _band_sample(q, k_cache_t, v_cache_t, k_new_t, v_new_t, bias, bias_new):
    b, t, d = q.shape
    w = k_cache_t.shape[2]
    n_heads = d // HEAD_DIM
    per = SAMPLE_KEY_PAD // t
    feat = lambda cols: pl.BlockSpec((1, d, cols), lambda i: (i, 0, 0))
    return pl.pallas_call(
        functools.partial(_band_sample_kernel, n_heads=n_heads),
        grid=(b,),
        in_specs=[pl.BlockSpec((1, t, d), lambda i: (i, 0, 0)), feat(w), feat(w),
                  _new_key_spec(d, t), _new_key_spec(d, t), _const_spec(bias.shape),
                  pl.BlockSpec((1,) + bias_new.shape[1:], lambda i: (i % per, 0, 0))],
        out_specs=pl.BlockSpec((1, t, d), lambda i: (i, 0, 0)),
        out_shape=jax.ShapeDtypeStruct((b, t, d), BF16),
        compiler_params=_params("parallel"),
        name="band_sample",
    )(q, k_cache_t, v_cache_t, k_new_t, v_new_t, bias, bias_new)


def _fox_sample_kernel(q_ref, kc_ref, vc_ref, kn_ref, vn_ref, ckc_ref, ckn_ref, cq_ref, o_ref,
                       qs_ref, m_ref, l_ref, acc_ref, s_ref, *, n_heads, t):
    j = pl.program_id(1)
    rows = n_heads * t
    sub = s_ref.shape[2]
    n_sub = kc_ref.shape[2] // sub

    @pl.when(j == 0)
    def _():
        qs_ref[...] = _stack_heads(q_ref[0], n_heads)
        m_ref[...] = jnp.full(m_ref.shape, NEG_INF, F32)
        l_ref[...] = jnp.zeros(l_ref.shape, F32)
        acc_ref[...] = jnp.zeros(acc_ref.shape, F32)

    def key_cumsum(ck_ref, cols):
        width = cols.stop - cols.start
        return jnp.concatenate([jnp.broadcast_to(ck_ref[0, h:h + 1, cols], (t, width)) for h in range(n_heads)], axis=0)

    def update(s, v_t):
        m, l, acc = _softmax_step((m_ref[...], l_ref[...], acc_ref[...]), s, v_t)
        m_ref[...] = m
        l_ref[...] = l
        acc_ref[...] = acc

    cq = cq_ref[0]

    def scores(u):
        cols = slice(u * sub, (u + 1) * sub)
        s_ref[u % 2] = (_dot(qs_ref[...], kc_ref[0, :, cols].astype(BF16))
                        + (cq - key_cumsum(ckc_ref, cols)) * LOG2E)

    scores(0)
    for u in range(n_sub):
        if u + 1 < n_sub:
            scores(u + 1)
        update(s_ref[u % 2], vc_ref[0, :, u * sub:(u + 1) * sub].astype(BF16))

    @pl.when(j == pl.num_programs(1) - 1)
    def _():
        s_n = (_dot(qs_ref[...], kn_ref[0].astype(BF16))
               + (cq - key_cumsum(ckn_ref, slice(0, SAMPLE_KEY_PAD))) * LOG2E)
        q_pos = lax.broadcasted_iota(jnp.int32, (rows, SAMPLE_KEY_PAD), 0) % t
        col = lax.broadcasted_iota(jnp.int32, (rows, SAMPLE_KEY_PAD), 1)
        own = col // t == pl.program_id(0) % (SAMPLE_KEY_PAD // t)
        update(jnp.where(own & (col % t <= q_pos), s_n, NEG_INF), vn_ref[0].astype(BF16))
        o_ref[0] = _unstack_heads(acc_ref[...] / l_ref[...], n_heads).astype(o_ref.dtype)


def _fox_sample(q, k_cache_t, v_cache_t, k_new_t, v_new_t, c_keys, c_new, c_q):
    b, t, d = q.shape
    past = k_cache_t.shape[2]
    n_heads = d // HEAD_DIM
    kb = min(FOX_S_KBLOCK, past)
    sub = min(FOX_S_SUB, kb)
    assert past % kb == 0 and kb % sub == 0 and sub % SAMPLE_KEY_PAD == 0 and SAMPLE_KEY_PAD % t == 0
    rows = n_heads * t
    per_b = lambda shape: pl.BlockSpec((1,) + shape, lambda i, j: (i, 0, 0))
    cache = pl.BlockSpec((1, d, kb), lambda i, j: (i, 0, j))
    return pl.pallas_call(
        functools.partial(_fox_sample_kernel, n_heads=n_heads, t=t),
        grid=(b, past // kb),
        in_specs=[per_b((t, d)), cache, cache, _new_key_spec(d, t), _new_key_spec(d, t),
                  pl.BlockSpec((1, n_heads, kb), lambda i, j: (i, 0, j)),
                  per_b((n_heads, SAMPLE_KEY_PAD)), per_b((rows, 1))],
        out_specs=per_b((t, d)),
        out_shape=jax.ShapeDtypeStruct((b, t, d), BF16),
        scratch_shapes=[pltpu.VMEM((rows, d), BF16), pltpu.VMEM((rows, 1), F32),
                        pltpu.VMEM((rows, 1), F32), pltpu.VMEM((rows, d), F32),
                        pltpu.VMEM((2, rows, sub), F32)],
        compiler_params=_params("parallel", "arbitrary"),
        name="fox_sample",
    )(q, k_cache_t, v_cache_t, k_new_t, v_new_t, c_keys, c_new, c_q)


def _tail_kernel(x_ref, oa_ref, ob_ref, p_ref, woa_ref, wob_ref, nf_ref, wg_ref, wu_ref, wd_ref,
                 np_ref, wpg_ref, wpp_ref, y_ref, *, n_ff_chunks):
    x = x_ref[...] + _dot(oa_ref[...], woa_ref[...]) + _dot(ob_ref[...], wob_ref[...])
    h = _rms(x, nf_ref[...]).astype(BF16)
    ffn = None
    for c in range(n_ff_chunks):
        cols = slice(c * FF_CHUNK, (c + 1) * FF_CHUNK)
        g = _dot(h, wg_ref[:, cols])
        u = _dot(h, wu_ref[:, cols])
        part = _dot((g * jax.nn.sigmoid(g) * u).astype(BF16), wd_ref[cols, :])
        ffn = part if ffn is None else ffn + part
    x = x + ffn
    gate = jax.nn.sigmoid(_dot(_rms(x, np_ref[...]).astype(BF16), wpg_ref[...]))
    y_ref[...] = x + _dot(p_ref[...].astype(BF16), wpp_ref[...]) * gate


def _tail(x, oa, ob, p, woa, wob, nf, wg, wu, wd, npl, wpg, wpp):
    n, d = x.shape
    d_ff = wg.shape[1]
    tm = min(ROW_TILE, n)
    assert n % tm == 0 and d_ff % FF_CHUNK == 0
    row = lambda a: pl.BlockSpec((tm, a.shape[1]), lambda i: (i, 0))
    weights = (woa, wob, nf, wg, wu, wd, npl, wpg, wpp)
    return pl.pallas_call(
        functools.partial(_tail_kernel, n_ff_chunks=d_ff // FF_CHUNK),
        grid=(n // tm,),
        in_specs=[row(x), row(oa), row(ob), row(p)] + [_const_spec(w.shape) for w in weights],
        out_specs=row(x),
        out_shape=jax.ShapeDtypeStruct((n, d), F32),
        compiler_params=_params("parallel"),
        name="tail",
    )(x, oa, ob, p, *weights)


def _band_bias_gen(table):
    table = table.astype(F32) * LOG2E
    n = BAND_W + BAND_Q - 1
    clipped = jnp.concatenate([jnp.repeat(table[:, :1], n, axis=1), table, jnp.repeat(table[:, -1:], n, axis=1)],
                              axis=1)
    first = n + REL_CLIP + A_WINDOW - (BAND_W - 1)
    return jnp.concatenate([clipped[:, first:first + n], jnp.zeros((table.shape[0], 1), F32)], axis=1)


def _band_bias_sample(gen, st, n_keys):
    rev = gen[:, ::-1]
    rows = jnp.stack([rev[:, BAND_Q - t:BAND_Q - t + BAND_W] for t in range(st)], axis=1)
    return jnp.where(jnp.arange(BAND_W)[None, None, :] < n_keys, rows, NEG_INF)


def _feature_major(cache):
    b, t, n, hd = cache.shape
    return cache.transpose(0, 2, 3, 1).reshape(b, n * hd, t)


def _time_major(x_t, n_heads):
    b, _, t = x_t.shape
    return x_t.reshape(b, n_heads, HEAD_DIM, t).transpose(0, 3, 1, 2)


def kernel(x_prompt, x_sample, cache_k_a, cache_v_a, cache_k_b, cache_v_b, cache_logf_b, p_prompt, p_sample, norm_mix, w_in, b_f, q_norm_a, k_norm_a, q_norm_b, k_norm_b, rel_bias_a, w_out, norm_ffn, w_gate, w_up, w_down, norm_ple, w_ple_gate, w_ple_proj):
    depth = w_in.shape[0]
    b, t, d = x_prompt.shape
    sb, st, _ = x_sample.shape
    n_ha, n_hb = cache_k_a.shape[3], cache_k_b.shape[3]
    d_a, d_b = n_ha * HEAD_DIM, n_hb * HEAD_DIM
    past = cache_k_b.shape[2]
    w_a = cache_k_a.shape[2]
    n_s = sb * st
    assert d_a == d_b and w_a == A_WINDOW and st <= CHUNK and t >= A_WINDOW and n_s % ROW_TILE == 0
    assert SAMPLE_KEY_PAD % st == 0 and n_s % SAMPLE_KEY_PAD == 0
    n_groups = n_hb // HEADS_PER_GROUP

    xp = x_prompt
    xs = x_sample.reshape(1, n_s, d)
    outs = [[] for _ in range(10)]
    for i in range(depth):
        wt = w_in[i].T.astype(BF16)
        g_mix = norm_mix[i].reshape(1, d)
        bf_t = jnp.broadcast_to(b_f[i][:, None], (n_hb, ROW_TILE)).astype(F32)
        gains_t = jnp.stack([jnp.broadcast_to(g[i][:, None], (HEAD_DIM, ROW_TILE))
                             for g in (q_norm_a, k_norm_a, q_norm_b, k_norm_b)]).astype(F32)
        tail_w = (w_out[i, :d_a].astype(BF16), w_out[i, d_a:].astype(BF16), norm_ffn[i].reshape(1, d),
                  w_gate[i].astype(BF16), w_up[i].astype(BF16), w_down[i].astype(BF16),
                  norm_ple[i].reshape(1, d), w_ple_gate[i].astype(BF16), w_ple_proj[i].astype(BF16))
        bias_gen = _band_bias_gen(rel_bias_a[i])

        qa_t, qb_t, ka16, kb16, va_t16, vb_t16, ka_t, va_t, kb_t, vb_t, lf_t = _proj(
            xp, g_mix, wt, bf_t, gains_t, n_hb, A_WINDOW)
        c = _cumsum_lanes(lf_t.reshape(b * n_hb, t)).reshape(b, n_groups, HEADS_PER_GROUP, t)
        o_a = _band_prompt(ka16, qa_t, va_t16, bias_gen.reshape(n_ha // HEADS_PER_GROUP, HEADS_PER_GROUP, -1))
        o_b = _fox_prompt(kb16, qb_t, vb_t16, c)
        xp = _tail(xp.reshape(b * t, d), o_a.reshape(b * t, d_a), o_b.reshape(b * t, d_b),
                   p_prompt[i].reshape(b * t, -1), *tail_w).reshape(b, t, d)
        outs[0].append(_time_major(ka_t, n_ha))
        outs[1].append(_time_major(va_t, n_ha))
        outs[2].append(_time_major(kb_t, n_hb))
        outs[3].append(_time_major(vb_t, n_hb))
        outs[4].append(lf_t.transpose(0, 2, 1))

        qa_t, qb_t, ka16, kb16, va_t16, vb_t16, ka_t, va_t, kb_t, vb_t, lf_t = _proj(
            xs, g_mix, wt, bf_t, gains_t, n_hb, n_s)
        per_stream = lambda x_t: x_t.reshape(-1, sb, st).transpose(1, 0, 2)
        unslab = lambda s_t: s_t[0].transpose(1, 0, 2).reshape(s_t.shape[2], n_s)
        q_rows = lambda s_t: unslab(s_t).T.reshape(sb, st, -1)
        lf_s = per_stream(lf_t[0])
        lf_all = jnp.concatenate([cache_logf_b[i].astype(F32).transpose(0, 2, 1), lf_s], axis=2)
        t_all = past + SAMPLE_KEY_PAD
        lf_all = jnp.pad(lf_all, ((0, 0), (0, 0), (0, t_all - past - st))).reshape(sb * n_hb, t_all)
        c_all = _cumsum_lanes(lf_all).reshape(sb, n_hb, t_all)
        c_new = c_all[:, :, past:past + st]
        c_q = c_new.reshape(sb, n_hb * st, 1)
        per = SAMPLE_KEY_PAD // st
        bias_s = _band_bias_sample(bias_gen, st, w_a + st).reshape(n_ha * st, BAND_W)
        own = (jnp.arange(SAMPLE_KEY_PAD)[None, None, :] // st) == jnp.arange(per)[:, None, None]
        bias_new = jnp.where(own, jnp.tile(bias_s[:, w_a:w_a + st], (1, per))[None], NEG_INF)
        o_a = _band_sample(q_rows(qa_t), _feature_major(cache_k_a[i]), _feature_major(cache_v_a[i]),
                           ka_t, va_t, bias_s[:, :w_a], bias_new)
        o_b = _fox_sample(q_rows(qb_t), _feature_major(cache_k_b[i]), _feature_major(cache_v_b[i]),
                          kb_t, vb_t, c_all, jnp.tile(c_new, (1, 1, per)), c_q)
        xs = _tail(xs.reshape(n_s, d), o_a.reshape(n_s, d_a), o_b.reshape(n_s, d_b),
                   p_sample[i].reshape(n_s, -1), *tail_w).reshape(1, n_s, d)
        outs[5].append(_time_major(per_stream(ka_t[0]), n_ha))
        outs[6].append(_time_major(per_stream(va_t[0]), n_ha))
        outs[7].append(_time_major(per_stream(kb_t[0]), n_hb))
        outs[8].append(_time_major(per_stream(vb_t[0]), n_hb))
        outs[9].append(lf_s.transpose(0, 2, 1))

    return (xp, xs.reshape(sb, st, d)) + tuple(jnp.stack(o) for o in outs)
```

```python
import functools

import jax
import jax.numpy as jnp
from jax import lax
from jax.experimental import pallas as pl
from jax.experimental.pallas import tpu as pltpu

F32 = jnp.float32
BF16 = jnp.bfloat16

CHUNK = 64
HEAD_DIM = 64
A_LEFT_CHUNKS = 8
A_WINDOW = A_LEFT_CHUNKS * CHUNK
REL_CLIP = 128
ATTN_SCALE = HEAD_DIM ** -0.5
LOG2E = 1.4426950408889634
Q_SCALE = ATTN_SCALE * LOG2E
NEG_INF = -1e30
EPS = 1e-6

LANES = 128
HEADS_PER_GROUP = LANES // HEAD_DIM
VMEM_LIMIT = 56 * 1024 * 1024

ROW_TILE = 512
PROJ_TILE = 1024
TAIL_TILE = 1024
TIME_BLOCK = 128
BAND_Q = 2 * CHUNK
BAND_PAIR = 2
BAND_W = A_WINDOW + BAND_Q
FOX_T = 512
FOX_H = FOX_T // 2
ATTN_GROUPS = 4
FF_CHUNK = 256
ONES_ROWS = 16
BIAS_ROWS = 16
N_SPLIT = 3
SAMPLE_KEY_PAD = 128
FOX_S_KBLOCK = 4096
FOX_S_SUB = 512


def _params(*sem):
    return pltpu.CompilerParams(dimension_semantics=sem, vmem_limit_bytes=VMEM_LIMIT)


def _const_spec(shape):
    zeros = (0,) * len(shape)
    return pl.BlockSpec(shape, lambda *_: zeros, pipeline_mode=pl.Buffered(1))


def _rms(x, gain):
    ms = jnp.mean(x * x, axis=-1, keepdims=True)
    return x * lax.rsqrt(ms + EPS) * gain


def _dot(a, b):
    return jnp.dot(a, b, preferred_element_type=F32)


def _dot_nt(a, b):
    return lax.dot_general(a, b, (((1,), (1,)), ((), ())), preferred_element_type=F32)


def _proj_kernel(x_ref, g_ref, wt_ref, bf_ref, gains_ref,
                 qa_ref, qb_ref, ka16_ref, kb16_ref, va16_ref, vb16_ref,
                 ka_ref, va_ref, kb_ref, vb_ref, lf_ref, *, d_seg, n_fh, sub):
    tm = x_ref.shape[1]
    n_heads = d_seg // HEAD_DIM
    slabs = sub // TIME_BLOCK
    kept_from = tm - ka_ref.shape[2]

    def head_norm(y, gi):
        rows = []
        for hd in range(n_heads):
            yh = y[hd * HEAD_DIM:(hd + 1) * HEAD_DIM, :]
            ms = jnp.sum(yh * yh, axis=0, keepdims=True) * (1.0 / HEAD_DIM)
            rows.append(yh * lax.rsqrt(ms + EPS) * gains_ref[gi])
        return jnp.concatenate(rows, axis=0)

    cols = [slice(u * sub, (u + 1) * sub) for u in range(tm // sub)]
    hs = [_rms(x_ref[0, c, :], g_ref[...]).astype(BF16) for c in cols]

    def seg(u, s, extra=0):
        return _dot_nt(wt_ref[s * d_seg:(s + 1) * d_seg + extra, :], hs[u])

    def put_slabs(ref, u, y):
        y = y.astype(BF16)
        for v in range(slabs):
            ref[0, u * slabs + v] = y[:, v * TIME_BLOCK:(v + 1) * TIME_BLOCK]

    def put_kept(ref, u, y):
        if (u + 1) * sub > kept_from:
            ref[0, :, u * sub - kept_from:(u + 1) * sub - kept_from] = y

    def q_a(u):
        put_slabs(qa_ref, u, head_norm(seg(u, 0), 0) * Q_SCALE)

    def k_a(u):
        ka = head_norm(seg(u, 1), 1)
        ka16_ref[0, cols[u], :] = ka.T.astype(BF16)
        put_kept(ka_ref, u, ka)

    def v_a(u):
        va = seg(u, 2)
        put_slabs(va16_ref, u, va)
        put_kept(va_ref, u, va)

    def q_b(u):
        put_slabs(qb_ref, u, head_norm(seg(u, 3), 2) * Q_SCALE)

    def k_b(u):
        kb = head_norm(seg(u, 4), 3)
        kb_ref[0, :, cols[u]] = kb
        kb16_ref[0, cols[u], :] = kb.T.astype(BF16)

    def v_b(u):
        tail = seg(u, 5, n_fh)
        vb = tail[:d_seg]
        vb_ref[0, :, cols[u]] = vb
        put_slabs(vb16_ref, u, vb)
        z = tail[d_seg:] + bf_ref[...]
        lf_ref[0, :, cols[u]] = jnp.minimum(z, 0.0) - jnp.log1p(jnp.exp(-jnp.abs(z)))

    for stage in (q_a, k_a, v_a, q_b, k_b, v_b):
        for u in range(len(cols)):
            stage(u)


def _proj(x, g, wt, bf_t, gains_t, n_fh, keep):
    b, t, d = x.shape
    d_seg = (wt.shape[0] - n_fh) // 6
    tm, sub = min(PROJ_TILE, t), ROW_TILE
    assert t % tm == 0 and tm % sub == 0 and sub % TIME_BLOCK == 0 and keep <= tm and keep % sub == 0
    nt, n_slabs = t // tm, tm // TIME_BLOCK
    feat = lambda rows: pl.BlockSpec((1, rows, tm), lambda i, j: (i, 0, j))
    slab = pl.BlockSpec((1, n_slabs, d_seg, TIME_BLOCK), lambda i, j: (i, j, 0, 0))
    kept = pl.BlockSpec((1, d_seg, keep), lambda i, j: (i, 0, 0))
    std = pl.BlockSpec((1, tm, d_seg), lambda i, j: (i, j, 0))
    slab_out = jax.ShapeDtypeStruct((b, t // TIME_BLOCK, d_seg, TIME_BLOCK), BF16)
    std_out = jax.ShapeDtypeStruct((b, t, d_seg), BF16)
    feat_out = jax.ShapeDtypeStruct((b, d_seg, t), F32)
    kept_out = jax.ShapeDtypeStruct((b, d_seg, keep), F32)
    return pl.pallas_call(
        functools.partial(_proj_kernel, d_seg=d_seg, n_fh=n_fh, sub=sub),
        grid=(b, nt),
        in_specs=[pl.BlockSpec((1, tm, d), lambda i, j: (i, j, 0)), _const_spec(g.shape), _const_spec(wt.shape),
                  _const_spec(bf_t.shape), _const_spec(gains_t.shape)],
        out_specs=[slab, slab, std, std, slab, slab, kept, kept, feat(d_seg), feat(d_seg), feat(n_fh)],
        out_shape=[slab_out, slab_out, std_out, std_out, slab_out, slab_out,
                   kept_out, kept_out, feat_out, feat_out, jax.ShapeDtypeStruct((b, n_fh, t), F32)],
        compiler_params=_params("parallel", "arbitrary"),
        name="proj",
    )(x, g, wt, bf_t, gains_t)


def _cumsum_kernel(x_ref, u_ref, c_ref, *, n_blocks):
    u = u_ref[...]
    off = jnp.zeros((x_ref.shape[0], 1), F32)
    for j in range(n_blocks):
        x = x_ref[:, j * LANES:(j + 1) * LANES]
        hi = x.astype(BF16)
        r1 = x - hi.astype(F32)
        mid = r1.astype(BF16)
        lo = (r1 - mid.astype(F32)).astype(BF16)
        c = (_dot(hi, u) + _dot(mid, u) + _dot(lo, u)) + off
        c_ref[:, j * LANES:(j + 1) * LANES] = c
        off = c[:, LANES - 1:LANES]


def _cumsum_lanes(x):
    g, t = x.shape
    assert t % LANES == 0
    idx = jnp.arange(LANES)
    u = (idx[:, None] <= idx[None, :]).astype(BF16)
    return pl.pallas_call(
        functools.partial(_cumsum_kernel, n_blocks=t // LANES),
        out_shape=jax.ShapeDtypeStruct((g, t), F32),
        compiler_params=pltpu.CompilerParams(vmem_limit_bytes=VMEM_LIMIT),
        name="cumsum",
    )(x, u)


def _pair_mask(tq):
    shape = (LANES, HEADS_PER_GROUP * tq)
    row_head = lax.broadcasted_iota(jnp.int32, shape, 0) // HEAD_DIM
    col_head = lax.broadcasted_iota(jnp.int32, shape, 1) // tq
    return (row_head == col_head).astype(F32).astype(BF16)


def _pair_diag(o2, tq):
    return jnp.concatenate([o2[h * HEAD_DIM:(h + 1) * HEAD_DIM, h * tq:(h + 1) * tq]
                            for h in range(HEADS_PER_GROUP)], axis=0)


def _v_dot_p(v_ref, blk0, p, rows=slice(None)):
    keys = p.shape[0]
    v = jnp.concatenate([v_ref[0, blk0 + u, rows, :] for u in range(keys // TIME_BLOCK)], axis=1)
    return _dot(jnp.concatenate([v, jnp.ones((ONES_ROWS, keys), BF16)], axis=0), p)


def _q_tile(q_ref, blk0, n_blk, rows=slice(None)):
    parts = [q_ref[0, blk0 + u, rows, :] for u in range(n_blk)]
    return parts[0] if n_blk == 1 else jnp.concatenate(parts, axis=1)


def _band_kernel(k_ref, qt_ref, vt_ref, gen_ref, o_ref, s_ref, bias_ref, *, n_tiles, n_groups):
    head_mask = _pair_mask(BAND_Q)
    q_blks = BAND_Q // TIME_BLOCK
    period = gen_ref.shape[2]

    @pl.when(pl.program_id(1) == 0)
    def _():
        j = lax.broadcasted_iota(jnp.int32, (BAND_W, BAND_Q), 0)
        r = lax.broadcasted_iota(jnp.int32, (BAND_W, BAND_Q), 1)
        gap = j // CHUNK - r // CHUNK
        in_band = (gap >= 0) & (gap <= A_LEFT_CHUNKS)
        for g in range(n_groups):
            for h in range(HEADS_PER_GROUP):
                rows = jnp.broadcast_to(gen_ref[g, h:h + 1, :], (BAND_W, period))
                toeplitz = pltpu.roll(rows, period - BAND_W + 1, 1, stride=1, stride_axis=0)[:, :BAND_Q]
                bias_ref[g, :, h * BAND_Q:(h + 1) * BAND_Q] = jnp.where(in_band, toeplitz, NEG_INF)

    def window(i):
        k0 = max(i * BAND_Q - A_WINDOW, 0)
        return k0, (i + 1) * BAND_Q - k0

    def scores(slot, g, n):
        lanes = slice(g * LANES, (g + 1) * LANES)
        maxes = []
        for a in range(BAND_PAIR):
            i = n * BAND_PAIR + a
            k0, width = window(i)
            q = _q_tile(qt_ref, i * q_blks, q_blks, lanes)
            q2 = jnp.concatenate([q] * HEADS_PER_GROUP, axis=1) * head_mask
            s = _dot(k_ref[0, k0:k0 + width, lanes], q2) + bias_ref[g, BAND_W - width:, :]
            s_ref[slot % 2, a, 0:width, :] = s
            maxes.append(jnp.max(s, axis=0, keepdims=True))
        return maxes

    def finish(slot, g, n, maxes):
        lanes = slice(g * LANES, (g + 1) * LANES)
        outs = []
        for a in range(BAND_PAIR):
            i = n * BAND_PAIR + a
            k0, width = window(i)
            p = jnp.exp2(s_ref[slot % 2, a, 0:width, :] - maxes[a])
            o2 = _v_dot_p(vt_ref, k0 // TIME_BLOCK, p.astype(BF16), lanes)
            outs.append(_pair_diag(o2[:LANES] / o2[LANES:LANES + 1], BAND_Q))
        rows = BAND_PAIR * BAND_Q
        o_ref[0, n * rows:(n + 1) * rows, lanes] = jnp.concatenate(outs, axis=1).T.astype(o_ref.dtype)

    steps = [(g, n) for g in range(n_groups) for n in range(n_tiles // BAND_PAIR)]
    maxes = scores(0, *steps[0])
    for slot, (g, n) in enumerate(steps):
        nxt = scores(slot + 1, *steps[slot + 1]) if slot + 1 < len(steps) else None
        finish(slot, g, n, maxes)
        maxes = nxt


def _band_prompt(k, q_t, v_t, gen):
    b, t, d = k.shape
    gs = min(ATTN_GROUPS, d // LANES)
    assert t % (BAND_PAIR * BAND_Q) == 0 and BAND_Q % TIME_BLOCK == 0 and A_WINDOW % TIME_BLOCK == 0
    assert d % (gs * LANES) == 0
    slab = pl.BlockSpec((1, t // TIME_BLOCK, gs * LANES, TIME_BLOCK), lambda j, i: (i, 0, j, 0))
    std = pl.BlockSpec((1, t, gs * LANES), lambda j, i: (i, 0, j))
    return pl.pallas_call(
        functools.partial(_band_kernel, n_tiles=t // BAND_Q, n_groups=gs),
        grid=(d // (gs * LANES), b),
        in_specs=[std, slab, slab,
                  pl.BlockSpec((gs, HEADS_PER_GROUP, gen.shape[2]), lambda j, i: (j, 0, 0))],
        out_specs=std,
        out_shape=jax.ShapeDtypeStruct((b, t, d), BF16),
        scratch_shapes=[pltpu.VMEM((2, BAND_PAIR, BAND_W, HEADS_PER_GROUP * BAND_Q), F32),
                        pltpu.VMEM((gs, BAND_W, HEADS_PER_GROUP * BAND_Q), F32)],
        compiler_params=_params("parallel", "arbitrary"),
        name="band_prompt",
    )(k, q_t, v_t, gen)


def _fox_kernel(k_ref, qt_ref, vt_ref, c_ref, o_ref, s_ref, kaug_ref, *, n_tiles, n_groups):
    n_halves = FOX_T // FOX_H
    n_cb = n_halves * HEADS_PER_GROUP
    blks = FOX_T // TIME_BLOCK
    wide = n_cb * FOX_H
    col_blocks = [slice(c * FOX_H, (c + 1) * FOX_H) for c in range(n_cb)]
    row_head = lax.broadcasted_iota(jnp.int32, (LANES, wide), 0) // HEAD_DIM
    col_head = (lax.broadcasted_iota(jnp.int32, (1, wide), 1) // FOX_H) % HEADS_PER_GROUP
    head_mask = (row_head == col_head).astype(F32).astype(BF16)
    key_pos = lax.broadcasted_iota(jnp.int32, (FOX_H, wide), 0)
    col = lax.broadcasted_iota(jnp.int32, (FOX_H, wide), 1)
    causal_lo = (key_pos <= col % FOX_H) | (col >= HEADS_PER_GROUP * FOX_H)
    causal_hi = causal_lo[:, :HEADS_PER_GROUP * FOX_H]

    ones_row0 = N_SPLIT * HEADS_PER_GROUP

    def split3(x):
        hi = x.astype(BF16).astype(F32)
        mid = (x - hi).astype(BF16).astype(F32)
        return hi, mid, (x - hi - mid).astype(BF16).astype(F32)

    def bias_rows(values, width):
        row = lax.broadcasted_iota(jnp.int32, (BIAS_ROWS, width), 0)
        top = jnp.zeros((BIAS_ROWS, width), F32)
        for r, v in enumerate(values):
            top = jnp.where(row == r, v, top)
        return jnp.concatenate([top, jnp.zeros((LANES - BIAS_ROWS, width), F32)], axis=0)

    for g in range(n_groups):
        for j in range(n_tiles):
            parts = split3(c_ref[0, g, :, j * FOX_T:(j + 1) * FOX_T] * (-LOG2E))
            values = [parts[r // HEADS_PER_GROUP][r % HEADS_PER_GROUP:r % HEADS_PER_GROUP + 1]
                      for r in range(ones_row0)]
            values += [jnp.ones((1, FOX_T), F32)] * N_SPLIT
            kaug_ref[g, j] = bias_rows(values, FOX_T).T.astype(BF16)

    q2_cache = {}

    def keys_of(c, diag):
        return FOX_H if diag and c < HEADS_PER_GROUP else FOX_T

    def scores(n, g, i, j):
        lanes = slice(g * LANES, (g + 1) * LANES)
        if (g, i) not in q2_cache:
            q = _q_tile(qt_ref, i * blks, blks, lanes)
            q = jnp.concatenate([q[:, (c // HEADS_PER_GROUP) * FOX_H:(c // HEADS_PER_GROUP + 1) * FOX_H]
                                 for c in range(n_cb)], axis=1)
            c0 = split3(c_ref[0, g, :, i * FOX_T:i * FOX_T + 1] * LOG2E)
            values = [(col_head == r % HEADS_PER_GROUP).astype(F32) for r in range(ones_row0)]
            values += [jnp.where(col_head == 0, part[0:1], part[1:2]) for part in c0]
            q2_cache.clear()
            q2_cache[g, i] = jnp.concatenate([q * head_mask, bias_rows(values, wide).astype(BF16)], axis=0)
        q2 = q2_cache[g, i]
        keys = jnp.concatenate([k_ref[0, j * FOX_T:(j + 1) * FOX_T, lanes], kaug_ref[g, j]], axis=1)
        if j < i:
            s = _dot(keys, q2)
            s_ref[n % 2] = s
            return [jnp.max(s[:, cb], axis=0, keepdims=True) for cb in col_blocks]
        upper = HEADS_PER_GROUP * FOX_H
        s_lo = jnp.where(causal_lo, _dot(keys[:FOX_H], q2), NEG_INF)
        s_hi = jnp.where(causal_hi, _dot(keys[FOX_H:], q2[:, upper:]), NEG_INF)
        s_ref[n % 2, 0:FOX_H, :] = s_lo
        s_ref[n % 2, FOX_H:, upper:] = s_hi
        maxes = [jnp.max(s_lo[:, cb], axis=0, keepdims=True) for cb in col_blocks]
        for c in range(HEADS_PER_GROUP, n_cb):
            hi_max = jnp.max(s_hi[:, col_blocks[c - HEADS_PER_GROUP]], axis=0, keepdims=True)
            maxes[c] = jnp.maximum(maxes[c], hi_max)
        return maxes

    def update(n, g, i, j, maxes, carry):
        new = []
        for c, (m, acc) in enumerate(carry):
            nk = keys_of(c, j == i)
            m_new = jnp.maximum(m, maxes[c])
            p = jnp.exp2(s_ref[n % 2, 0:nk, col_blocks[c]] - m_new).astype(BF16)
            row0 = g * LANES + (c % HEADS_PER_GROUP) * HEAD_DIM
            pv = _v_dot_p(vt_ref, j * blks, p, slice(row0, row0 + HEAD_DIM))
            new.append((m_new, jnp.exp2(m - m_new) * acc + pv))
        return new

    pairs = [(g, i, j) for g in range(n_groups) for i in range(n_tiles) for j in range(i + 1)]
    maxes = scores(0, *pairs[0])
    carry = None
    for n, (g, i, j) in enumerate(pairs):
        nxt = scores(n + 1, *pairs[n + 1]) if n + 1 < len(pairs) else None
        if j == 0:
            carry = [(jnp.full((1, FOX_H), NEG_INF, F32), jnp.zeros((HEAD_DIM + ONES_ROWS, FOX_H), F32))
                     for _ in range(n_cb)]
        carry = update(n, g, i, j, maxes, carry)
        maxes = nxt
        if j == i:
            accs = [jnp.concatenate([carry[HEADS_PER_GROUP * half + h][1] for half in range(n_halves)], axis=1)
                    for h in range(HEADS_PER_GROUP)]
            out = jnp.concatenate([acc[:HEAD_DIM] / acc[HEAD_DIM:HEAD_DIM + 1] for acc in accs], axis=0)
            o_ref[0, i * FOX_T:(i + 1) * FOX_T, g * LANES:(g + 1) * LANES] = out.T.astype(o_ref.dtype)


def _fox_prompt(k, q_t, v_t, c_rows):
    b, t, d = k.shape
    gs = min(ATTN_GROUPS, d // LANES)
    assert t % FOX_T == 0 and FOX_T % TIME_BLOCK == 0 and (N_SPLIT + 1) * HEADS_PER_GROUP <= BIAS_ROWS
    assert d % (gs * LANES) == 0 and FOX_T == 2 * FOX_H
    slab = pl.BlockSpec((1, t // TIME_BLOCK, gs * LANES, TIME_BLOCK), lambda i, j: (i, 0, j, 0))
    std = pl.BlockSpec((1, t, gs * LANES), lambda i, j: (i, 0, j))
    return pl.pallas_call(
        functools.partial(_fox_kernel, n_tiles=t // FOX_T, n_groups=gs),
        grid=(b, d // (gs * LANES)),
        in_specs=[std, slab, slab,
                  pl.BlockSpec((1, gs, HEADS_PER_GROUP, t), lambda i, j: (i, j, 0, 0))],
        out_specs=std,
        out_shape=jax.ShapeDtypeStruct((b, t, d), BF16),
        scratch_shapes=[pltpu.VMEM((2, FOX_T, HEADS_PER_GROUP * FOX_T), F32),
                        pltpu.VMEM((gs, t // FOX_T, FOX_T, LANES), BF16)],
        compiler_params=_params("parallel", "parallel"),
        name="fox_prompt",
    )(k, q_t, v_t, c_rows)


def _stack_heads(q, n_heads):
    lane_head = lax.broadcasted_iota(jnp.int32, (1, q.shape[1]), 1) // HEAD_DIM
    return jnp.concatenate([q * (lane_head == h).astype(q.dtype) for h in range(n_heads)], axis=0)


def _unstack_heads(o, n_heads):
    t = o.shape[0] // n_heads
    lane_head = lax.broadcasted_iota(jnp.int32, (1, o.shape[1]), 1) // HEAD_DIM
    out = o[0:t]
    for h in range(1, n_heads):
        out = jnp.where(lane_head == h, o[h * t:(h + 1) * t], out)
    return out


def _softmax_step(carry, s, v_t):
    m, l, acc = carry
    m_new = jnp.maximum(m, jnp.max(s, axis=-1, keepdims=True))
    alpha = jnp.exp2(m - m_new)
    p = jnp.exp2(s - m_new)
    l = alpha * l + jnp.sum(p, axis=-1, keepdims=True)
    acc = alpha * acc + _dot_nt(p.astype(BF16), v_t)
    return m_new, l, acc


def _band_sample_kernel(q_ref, kc_ref, vc_ref, kn_ref, vn_ref, bias_ref, bias_new_ref, o_ref, *, n_heads):
    w = kc_ref.shape[2]
    qs = _stack_heads(q_ref[0], n_heads)
    s_c = _dot(qs, kc_ref[0].astype(BF16)) + bias_ref[:, 0:w]
    s_n = _dot(qs, kn_ref[0].astype(BF16)) + bias_new_ref[0]
    m = jnp.maximum(jnp.max(s_c, axis=-1, keepdims=True), jnp.max(s_n, axis=-1, keepdims=True))
    p_c = jnp.exp2(s_c - m)
    p_n = jnp.exp2(s_n - m)
    l = jnp.sum(p_c, axis=-1, keepdims=True) + jnp.sum(p_n, axis=-1, keepdims=True)
    o = (_dot_nt(p_c.astype(BF16), vc_ref[0].astype(BF16)) + _dot_nt(p_n.astype(BF16), vn_ref[0].astype(BF16))) / l
    o_ref[0] = _unstack_heads(o, n_heads).astype(o_ref.dtype)


def _new_key_spec(d, t):
    per = SAMPLE_KEY_PAD // t
    return pl.BlockSpec((1, d, SAMPLE_KEY_PAD), lambda i, *_: (0, 0, i // per))


def _band_sample(q, k_cache_t, v_cache_t, k_new_t, v_new_t, bias, bias_new):
    b, t, d = q.shape
    w = k_cache_t.shape[2]
    n_heads = d // HEAD_DIM
    per = SAMPLE_KEY_PAD // t
    feat = lambda cols: pl.BlockSpec((1, d, cols), lambda i: (i, 0, 0))
    return pl.pallas_call(
        functools.partial(_band_sample_kernel, n_heads=n_heads),
        grid=(b,),
        in_specs=[pl.BlockSpec((1, t, d), lambda i: (i, 0, 0)), feat(w), feat(w),
                  _new_key_spec(d, t), _new_key_spec(d, t), _const_spec(bias.shape),
                  pl.BlockSpec((1,) + bias_new.shape[1:], lambda i: (i % per, 0, 0))],
        out_specs=pl.BlockSpec((1, t, d), lambda i: (i, 0, 0)),
        out_shape=jax.ShapeDtypeStruct((b, t, d), BF16),
        compiler_params=_params("parallel"),
        name="band_sample",
    )(q, k_cache_t, v_cache_t, k_new_t, v_new_t, bias, bias_new)


def _fox_sample_kernel(q_ref, kc_ref, vc_ref, kn_ref, vn_ref, ckc_ref, ckn_ref, cq_ref, o_ref,
                       qs_ref, m_ref, l_ref, acc_ref, s_ref, *, n_heads, t):
    j = pl.program_id(1)
    rows = n_heads * t
    sub = s_ref.shape[2]
    n_sub = kc_ref.shape[2] // sub

    @pl.when(j == 0)
    def _():
        qs_ref[...] = _stack_heads(q_ref[0], n_heads)
        m_ref[...] = jnp.full(m_ref.shape, NEG_INF, F32)
        l_ref[...] = jnp.zeros(l_ref.shape, F32)
        acc_ref[...] = jnp.zeros(acc_ref.shape, F32)

    def key_cumsum(ck_ref, cols):
        width = cols.stop - cols.start
        return jnp.concatenate([jnp.broadcast_to(ck_ref[0, h:h + 1, cols], (t, width)) for h in range(n_heads)], axis=0)

    def update(s, v_t):
        m, l, acc = _softmax_step((m_ref[...], l_ref[...], acc_ref[...]), s, v_t)
        m_ref[...] = m
        l_ref[...] = l
        acc_ref[...] = acc

    cq = cq_ref[0]

    def scores(u):
        cols = slice(u * sub, (u + 1) * sub)
        s_ref[u % 2] = (_dot(qs_ref[...], kc_ref[0, :, cols].astype(BF16))
                        + (cq - key_cumsum(ckc_ref, cols)) * LOG2E)

    scores(0)
    for u in range(n_sub):
        if u + 1 < n_sub:
            scores(u + 1)
        update(s_ref[u % 2], vc_ref[0, :, u * sub:(u + 1) * sub].astype(BF16))

    @pl.when(j == pl.num_programs(1) - 1)
    def _():
        s_n = (_dot(qs_ref[...], kn_ref[0].astype(BF16))
               + (cq - key_cumsum(ckn_ref, slice(0, SAMPLE_KEY_PAD))) * LOG2E)
        q_pos = lax.broadcasted_iota(jnp.int32, (rows, SAMPLE_KEY_PAD), 0) % t
        col = lax.broadcasted_iota(jnp.int32, (rows, SAMPLE_KEY_PAD), 1)
        own = col // t == pl.program_id(0) % (SAMPLE_KEY_PAD // t)
        update(jnp.where(own & (col % t <= q_pos), s_n, NEG_INF), vn_ref[0].astype(BF16))
        o_ref[0] = _unstack_heads(acc_ref[...] / l_ref[...], n_heads).astype(o_ref.dtype)


def _fox_sample(q, k_cache_t, v_cache_t, k_new_t, v_new_t, c_keys, c_new, c_q):
    b, t, d = q.shape
    past = k_cache_t.shape[2]
    n_heads = d // HEAD_DIM
    kb = min(FOX_S_KBLOCK, past)
    sub = min(FOX_S_SUB, kb)
    assert past % kb == 0 and kb % sub == 0 and sub % SAMPLE_KEY_PAD == 0 and SAMPLE_KEY_PAD % t == 0
    rows = n_heads * t
    per_b = lambda shape: pl.BlockSpec((1,) + shape, lambda i, j: (i, 0, 0))
    cache = pl.BlockSpec((1, d, kb), lambda i, j: (i, 0, j))
    return pl.pallas_call(
        functools.partial(_fox_sample_kernel, n_heads=n_heads, t=t),
        grid=(b, past // kb),
        in_specs=[per_b((t, d)), cache, cache, _new_key_spec(d, t), _new_key_spec(d, t),
                  pl.BlockSpec((1, n_heads, kb), lambda i, j: (i, 0, j)),
                  per_b((n_heads, SAMPLE_KEY_PAD)), per_b((rows, 1))],
        out_specs=per_b((t, d)),
        out_shape=jax.ShapeDtypeStruct((b, t, d), BF16),
        scratch_shapes=[pltpu.VMEM((rows, d), BF16), pltpu.VMEM((rows, 1), F32),
                        pltpu.VMEM((rows, 1), F32), pltpu.VMEM((rows, d), F32),
                        pltpu.VMEM((2, rows, sub), F32)],
        compiler_params=_params("parallel", "arbitrary"),
        name="fox_sample",
    )(q, k_cache_t, v_cache_t, k_new_t, v_new_t, c_keys, c_new, c_q)


def _tail_kernel(x_ref, oa_ref, ob_ref, p_ref, woa_ref, wob_ref, nf_ref, wg_ref, wu_ref, wd_ref,
                 np_ref, wpg_ref, wpp_ref, y_ref, *, n_ff_chunks, sub):
    tiles = [slice(t * sub, (t + 1) * sub) for t in range(x_ref.shape[0] // sub)]
    xs = [x_ref[r, :] + _dot(oa_ref[r, :], woa_ref[...]) + _dot(ob_ref[r, :], wob_ref[...]) for r in tiles]
    hs = [_rms(x, nf_ref[...]).astype(BF16) for x in xs]
    ffn = [None] * len(tiles)
    for c in range(n_ff_chunks):
        cols = slice(c * FF_CHUNK, (c + 1) * FF_CHUNK)
        for t, h in enumerate(hs):
            g = _dot(h, wg_ref[:, cols])
            u = _dot(h, wu_ref[:, cols])
            part = _dot((g * jax.nn.sigmoid(g) * u).astype(BF16), wd_ref[cols, :])
            ffn[t] = part if ffn[t] is None else ffn[t] + part
    xs = [x + f for x, f in zip(xs, ffn)]
    gates = [jax.nn.sigmoid(_dot(_rms(x, np_ref[...]).astype(BF16), wpg_ref[...])) for x in xs]
    for r, x, gate in zip(tiles, xs, gates):
        y_ref[r, :] = x + _dot(p_ref[r, :].astype(BF16), wpp_ref[...]) * gate


def _tail(x, oa, ob, p, woa, wob, nf, wg, wu, wd, npl, wpg, wpp):
    n, d = x.shape
    d_ff = wg.shape[1]
    tm, sub = min(TAIL_TILE, n), ROW_TILE
    assert n % tm == 0 and tm % sub == 0 and d_ff % FF_CHUNK == 0
    row = lambda a: pl.BlockSpec((tm, a.shape[1]), lambda i: (i, 0))
    weights = (woa, wob, nf, wg, wu, wd, npl, wpg, wpp)
    return pl.pallas_call(
        functools.partial(_tail_kernel, n_ff_chunks=d_ff // FF_CHUNK, sub=sub),
        grid=(n // tm,),
        in_specs=[row(x), row(oa), row(ob), row(p)] + [_const_spec(w.shape) for w in weights],
        out_specs=row(x),
        out_shape=jax.ShapeDtypeStruct((n, d), F32),
        compiler_params=_params("parallel"),
        name="tail",
    )(x, oa, ob, p, *weights)


def _band_bias_gen(table):
    table = table.astype(F32) * LOG2E
    n = BAND_W + BAND_Q - 1
    clipped = jnp.concatenate([jnp.repeat(table[:, :1], n, axis=1), table, jnp.repeat(table[:, -1:], n, axis=1)],
                              axis=1)
    first = n + REL_CLIP + A_WINDOW - (BAND_W - 1)
    return jnp.concatenate([clipped[:, first:first + n], jnp.zeros((table.shape[0], 1), F32)], axis=1)


def _band_bias_sample(gen, st, n_keys):
    rev = gen[:, ::-1]
    rows = jnp.stack([rev[:, BAND_Q - t:BAND_Q - t + BAND_W] for t in range(st)], axis=1)
    return jnp.where(jnp.arange(BAND_W)[None, None, :] < n_keys, rows, NEG_INF)


def _feature_major(cache):
    b, t, n, hd = cache.shape
    return cache.transpose(0, 2, 3, 1).reshape(b, n * hd, t)


def _time_major(x_t, n_heads):
    b, _, t = x_t.shape
    return x_t.reshape(b, n_heads, HEAD_DIM, t).transpose(0, 3, 1, 2)


def kernel(x_prompt, x_sample, cache_k_a, cache_v_a, cache_k_b, cache_v_b, cache_logf_b, p_prompt, p_sample, norm_mix, w_in, b_f, q_norm_a, k_norm_a, q_norm_b, k_norm_b, rel_bias_a, w_out, norm_ffn, w_gate, w_up, w_down, norm_ple, w_ple_gate, w_ple_proj):
    depth = w_in.shape[0]
    b, t, d = x_prompt.shape
    sb, st, _ = x_sample.shape
    n_ha, n_hb = cache_k_a.shape[3], cache_k_b.shape[3]
    d_a, d_b = n_ha * HEAD_DIM, n_hb * HEAD_DIM
    past = cache_k_b.shape[2]
    w_a = cache_k_a.shape[2]
    n_s = sb * st
    assert d_a == d_b and w_a == A_WINDOW and st <= CHUNK and t >= A_WINDOW and n_s % ROW_TILE == 0
    assert SAMPLE_KEY_PAD % st == 0 and n_s % SAMPLE_KEY_PAD == 0
    n_groups = n_hb // HEADS_PER_GROUP

    xp = x_prompt
    xs = x_sample.reshape(1, n_s, d)
    outs = [[] for _ in range(10)]
    for i in range(depth):
        wt = w_in[i].T.astype(BF16)
        g_mix = norm_mix[i].reshape(1, d)
        bf_t = jnp.broadcast_to(b_f[i][:, None], (n_hb, ROW_TILE)).astype(F32)
        gains_t = jnp.stack([jnp.broadcast_to(g[i][:, None], (HEAD_DIM, ROW_TILE))
                             for g in (q_norm_a, k_norm_a, q_norm_b, k_norm_b)]).astype(F32)
        tail_w = (w_out[i, :d_a].astype(BF16), w_out[i, d_a:].astype(BF16), norm_ffn[i].reshape(1, d),
                  w_gate[i].astype(BF16), w_up[i].astype(BF16), w_down[i].astype(BF16),
                  norm_ple[i].reshape(1, d), w_ple_gate[i].astype(BF16), w_ple_proj[i].astype(BF16))
        bias_gen = _band_bias_gen(rel_bias_a[i])

        qa_t, qb_t, ka16, kb16, va_t16, vb_t16, ka_t, va_t, kb_t, vb_t, lf_t = _proj(
            xp, g_mix, wt, bf_t, gains_t, n_hb, A_WINDOW)
        c = _cumsum_lanes(lf_t.reshape(b * n_hb, t)).reshape(b, n_groups, HEADS_PER_GROUP, t)
        o_a = _band_prompt(ka16, qa_t, va_t16, bias_gen.reshape(n_ha // HEADS_PER_GROUP, HEADS_PER_GROUP, -1))
        o_b = _fox_prompt(kb16, qb_t, vb_t16, c)
        xp = _tail(xp.reshape(b * t, d), o_a.reshape(b * t, d_a), o_b.reshape(b * t, d_b),
                   p_prompt[i].reshape(b * t, -1), *tail_w).reshape(b, t, d)
        outs[0].append(_time_major(ka_t, n_ha))
        outs[1].append(_time_major(va_t, n_ha))
        outs[2].append(_time_major(kb_t, n_hb))
        outs[3].append(_time_major(vb_t, n_hb))
        outs[4].append(lf_t.transpose(0, 2, 1))

        qa_t, qb_t, ka16, kb16, va_t16, vb_t16, ka_t, va_t, kb_t, vb_t, lf_t = _proj(
            xs, g_mix, wt, bf_t, gains_t, n_hb, n_s)
        per_stream = lambda x_t: x_t.reshape(-1, sb, st).transpose(1, 0, 2)
        unslab = lambda s_t: s_t[0].transpose(1, 0, 2).reshape(s_t.shape[2], n_s)
        q_rows = lambda s_t: unslab(s_t).T.reshape(sb, st, -1)
        lf_s = per_stream(lf_t[0])
        lf_all = jnp.concatenate([cache_logf_b[i].astype(F32).transpose(0, 2, 1), lf_s], axis=2)
        t_all = past + SAMPLE_KEY_PAD
        lf_all = jnp.pad(lf_all, ((0, 0), (0, 0), (0, t_all - past - st))).reshape(sb * n_hb, t_all)
        c_all = _cumsum_lanes(lf_all).reshape(sb, n_hb, t_all)
        c_new = c_all[:, :, past:past + st]
        c_q = c_new.reshape(sb, n_hb * st, 1)
        per = SAMPLE_KEY_PAD // st
        bias_s = _band_bias_sample(bias_gen, st, w_a + st).reshape(n_ha * st, BAND_W)
        own = (jnp.arange(SAMPLE_KEY_PAD)[None, None, :] // st) == jnp.arange(per)[:, None, None]
        bias_new = jnp.where(own, jnp.tile(bias_s[:, w_a:w_a + st], (1, per))[None], NEG_INF)
        o_a = _band_sample(q_rows(qa_t), _feature_major(cache_k_a[i]), _feature_major(cache_v_a[i]),
                           ka_t, va_t, bias_s[:, :w_a], bias_new)
        o_b = _fox_sample(q_rows(qb_t), _feature_major(cache_k_b[i]), _feature_major(cache_v_b[i]),
                          kb_t, vb_t, c_all, jnp.tile(c_new, (1, 1, per)), c_q)
        xs = _tail(xs.reshape(n_s, d), o_a.reshape(n_s, d_a), o_b.reshape(n_s, d_b),
                   p_sample[i].reshape(n_s, -1), *tail_w).reshape(1, n_s, d)
        outs[5].append(_time_major(per_stream(ka_t[0]), n_ha))
        outs[6].append(_time_major(per_stream(va_t[0]), n_ha))
        outs[7].append(_time_major(per_stream(kb_t[0]), n_hb))
        outs[8].append(_time_major(per_stream(vb_t[0]), n_hb))
        outs[9].append(lf_s.transpose(0, 2, 1))

    return (xp, xs.reshape(sb, st, d)) + tuple(jnp.stack(o) for o in outs)
```
